```python
import math
import jax
import jax.numpy as jnp
from jax import lax
import numpy as np

D_MODEL = 4096
BATCH = 1
SEQ = 8192
DEPTH = 1
DEC_BATCH = 128
DEC_SEQ = 4
PAST_LEN = 2048
PAGE_SIZE = 128

H_A = 16
DH_A = 128
W_A = H_A * DH_A
H_IDX = 32
D_IDX = 128
R_IDX = 512
IDX_ROPE = 64
IDX_W_SCALE = (H_IDX ** -0.5) * (D_IDX ** -0.5)
TOPK_MAX = 256
Q_BLOCK = 128
H_B = 8
DK_B = 128
DV_B = 128
W_B = H_B * DV_B
CHUNK_B = 32
N_MEM = 256
H_M = 4
DH_M = 256
W_M = H_M * DH_M
N_EXP = 64
TOP_K = 8
N_GROUP = 8
TOPK_GROUP = 4
F_EXP = 1024
F_SH = 1024
ROUTED_SCALE = 2.5
ROPE_THETA = 10000.0
LN_EPS = 1e-5
N_BRANCH = 3
DN_ALPHA = (2.0 * DEPTH) ** 0.25
DN_BETA = (8.0 * DEPTH) ** -0.25
F32 = jnp.float32
SPLITS = (W_A, W_A, W_A, R_IDX, D_IDX, H_IDX, H_B * DK_B, W_B, H_B * DK_B, W_B, W_M, N_BRANCH * D_MODEL)
N_IN = sum(SPLITS)
SPLIT_POINTS = tuple(int(v) for v in np.cumsum(SPLITS)[:-1])

kernel_name = 'dsa_hgrn2_memx_moe_step'


def layer_norm(x, g, b):
    xf = x.astype(F32)
    mu = jnp.mean(xf, axis=-1, keepdims=True)
    var = jnp.mean(jnp.square(xf - mu), axis=-1, keepdims=True)
    return ((xf - mu) * lax.rsqrt(var + LN_EPS) * g.astype(F32) + b.astype(F32)).astype(x.dtype)


def rope(x, pos, rot_dim):
    half = rot_dim // 2
    inv = ROPE_THETA ** (-jnp.arange(half, dtype=F32) / half)
    ang = pos.astype(F32)[:, None] * inv[None, :]
    cos = jnp.cos(ang)[:, None, :]
    sin = jnp.sin(ang)[:, None, :]
    xr = x[..., :rot_dim].astype(F32)
    x1, x2 = xr[..., :half], xr[..., half:]
    rot = jnp.concatenate([x1 * cos - x2 * sin, x2 * cos + x1 * sin], axis=-1).astype(x.dtype)
    return jnp.concatenate([rot, x[..., rot_dim:]], axis=-1)


def front(x, pos, w_in, w_idx_qb, idx_ln_g, idx_ln_b):
    B, T, _ = x.shape
    z = jnp.einsum('btd,dn->btn', x, w_in)
    (qa, ka, va, c_idx, k_idx, w_idx, fb, ib, qb, gb, qm, gate_raw) = jnp.split(z, SPLIT_POINTS, axis=-1)
    qa = rope(qa.reshape(B, T, H_A, DH_A), pos, DH_A)
    ka = rope(ka.reshape(B, T, H_A, DH_A), pos, DH_A)
    va = va.reshape(B, T, H_A, DH_A)
    q_idx = rope(jnp.einsum('btr,rn->btn', c_idx, w_idx_qb).reshape(B, T, H_IDX, D_IDX), pos, IDX_ROPE)
    k_idx = rope(layer_norm(k_idx, idx_ln_g, idx_ln_b)[:, :, None, :], pos, IDX_ROPE)[:, :, 0, :]
    w_idx = w_idx * IDX_W_SCALE
    return (qa, ka, va, q_idx, k_idx, w_idx,
            fb.reshape(B, T, H_B, DK_B), ib.reshape(B, T, H_B, DV_B),
            qb.reshape(B, T, H_B, DK_B), gb.reshape(B, T, H_B, DV_B),
            qm.reshape(B, T, H_M, DH_M), gate_raw)


def indexer_scores(q_idx, w_idx, k_idx):
    s = jax.nn.relu(jnp.einsum('bthd,bsd->bths', q_idx, k_idx).astype(F32))
    return jnp.einsum('bths,bth->bts', s, w_idx.astype(F32))


def select_keys(scores, q_pos, k_pos, n_sel):
    admissible = k_pos[None, None, :] <= q_pos[None, :, None]
    vals, idx = lax.top_k(jnp.where(admissible, scores, -jnp.inf), n_sel)
    return idx, jnp.isfinite(vals)


def dsa_prompt(q, k, v, q_idx, w_idx, k_idx):
    B, L = q.shape[0], q.shape[1]
    n_sel = min(TOPK_MAX, L // 4)
    nb = L // Q_BLOCK
    k_pos = jnp.arange(L)

    def blk(args):
        qb, qib, wb, start = args
        q_pos = start + jnp.arange(Q_BLOCK)
        idx, valid = select_keys(indexer_scores(qib, wb, k_idx), q_pos, k_pos, n_sel)
        kg = jax.vmap(lambda a, i: a[i])(k, idx)
        vg = jax.vmap(lambda a, i: a[i])(v, idx)
        logits = jnp.einsum('bqhd,bqnhd->bqhn', qb, kg).astype(F32) * (DH_A ** -0.5)
        logits = jnp.where(valid[:, :, None, :], logits, -jnp.inf)
        p = jax.nn.softmax(logits, axis=-1).astype(v.dtype)
        return jnp.einsum('bqhn,bqnhd->bqhd', p, vg)

    def to_blocks(a):
        return jnp.moveaxis(a.reshape(B, nb, Q_BLOCK, *a.shape[2:]), 1, 0)

    out = lax.map(blk, (to_blocks(q), to_blocks(q_idx), to_blocks(w_idx), jnp.arange(nb) * Q_BLOCK))
    return jnp.moveaxis(out, 0, 1).reshape(B, L, H_A, DH_A)


def dsa_sample(q, k_new, v_new, q_idx, w_idx, k_idx_new, cache_k, cache_v, cache_idx_k, page_table, layer):
    DB, T = q.shape[0], q.shape[1]
    L = PAST_LEN + T
    n_sel = min(TOPK_MAX, L // 4)
    past_idx_k = cache_idx_k[layer, page_table].reshape(DB, PAST_LEN, D_IDX)
    k_idx_all = jnp.concatenate([past_idx_k, k_idx_new.astype(past_idx_k.dtype)], axis=1)
    new_pos = PAST_LEN + jnp.arange(T)
    idx, valid = select_keys(indexer_scores(q_idx, w_idx, k_idx_all), new_pos, jnp.arange(L), n_sel)
    p_idx = jnp.minimum(idx, PAST_LEN - 1)
    phys = jax.vmap(lambda pt, i: pt[i])(page_table, p_idx // PAGE_SIZE)
    off = p_idx % PAGE_SIZE
    kp = cache_k[layer, phys, off]
    vp = cache_v[layer, phys, off]
    past_ok = valid & (idx < PAST_LEN)
    sel_new = jnp.any((idx[..., None] == new_pos) & valid[..., None], axis=2)
    scale = DH_A ** -0.5
    lp = jnp.einsum('bqhd,bqnhd->bqhn', q, kp).astype(F32) * scale
    lp = jnp.where(past_ok[:, :, None, :], lp, -jnp.inf)
    ln = jnp.einsum('bqhd,bkhd->bqhk', q, k_new).astype(F32) * scale
    ln = jnp.where(sel_new[:, :, None, :], ln, -jnp.inf)
    p = jax.nn.softmax(jnp.concatenate([lp, ln], axis=-1), axis=-1)
    pp = p[..., :n_sel].astype(vp.dtype)
    pn = p[..., n_sel:].astype(v_new.dtype)
    out = jnp.einsum('bqhn,bqnhd->bqhd', pp, vp).astype(q.dtype) + jnp.einsum('bqhk,bkhd->bqhd', pn, v_new)
    return out


def hgrn_lower_bound(lb_logits, layer):
    p = jax.nn.softmax(lb_logits.astype(F32), axis=0)
    return jnp.cumsum(p, axis=0)[layer].reshape(H_B, DK_B)


def hgrn2(f_raw, i_in, q, g_out, state, lb, norm_g):
    B, L = q.shape[0], q.shape[1]
    C = math.gcd(L, CHUNK_B)
    n = L // C
    fr = f_raw.astype(F32)
    log_f = jnp.log(lb + (1.0 - lb) * jax.nn.sigmoid(fr))
    k = (1.0 - lb) * jax.nn.sigmoid(-fr)

    def chunks(a):
        return jnp.moveaxis(a.reshape(B, n, C, *a.shape[2:]), 1, 0)

    causal = jnp.tril(jnp.ones((C, C), dtype=bool))[None, :, :, None, None]

    def step(S, xs):
        qc, kc, ic, gc = xs
        G = jnp.cumsum(gc, axis=1)
        o_inter = jnp.einsum('bthk,bhkv->bthv', qc * jnp.exp(G), S)
        decay = jnp.exp(jnp.where(causal, G[:, :, None] - G[:, None, :], -jnp.inf))
        A = jnp.einsum('bthk,btshk,bshk->bhts', qc, decay, kc)
        o_intra = jnp.einsum('bhts,bshv->bthv', A, ic)
        G_last = G[:, -1]
        S_new = (jnp.exp(G_last)[..., None] * S
                 + jnp.einsum('bshk,bshv->bhkv', kc * jnp.exp(G_last[:, None] - G), ic))
        return S_new, o_inter + o_intra

    S_fin, o = lax.scan(step, state.astype(F32),
                        (chunks(q.astype(F32)), chunks(k), chunks(i_in.astype(F32)), chunks(log_f)))
    o = jnp.moveaxis(o, 0, 1).reshape(B, L, H_B, DV_B)
    o = (o * lax.rsqrt(jnp.mean(jnp.square(o), axis=-1, keepdims=True) + LN_EPS)
         * norm_g.astype(F32) * jax.nn.silu(g_out.astype(F32)))
    return o.reshape(B, L, W_B).astype(q.dtype), S_fin


def mem_kv(mem, w_kv):
    B, M, _ = mem.shape
    mk, mv = jnp.split(jnp.einsum('bmd,dn->bmn', mem, w_kv), 2, axis=-1)
    return mk.reshape(B, M, H_M, DH_M), mv.reshape(B, M, H_M, DH_M)


def mem_attend(q, mk, mv):
    logits = jnp.einsum('bthd,bmhd->bhtm', q, mk).astype(F32) * (DH_M ** -0.5)
    p = jax.nn.softmax(logits, axis=-1).astype(mv.dtype)
    return jnp.einsum('bhtm,bmhd->bthd', p, mv).astype(q.dtype)


def swiglu(x, w_up, w_down):
    a, b = jnp.split(x @ w_up, 2, axis=-1)
    return (jax.nn.silu(a) * b) @ w_down


def moe(xf, layer, w_router, router_bias, w_exp_in, w_exp_out, w_sh_in, w_sh_out):
    s = jax.nn.sigmoid(jnp.einsum('nd,de->ne', xf, w_router[layer]).astype(F32))
    choice = s + router_bias[layer].astype(F32)
    grp_score = jnp.sum(lax.top_k(choice.reshape(-1, N_GROUP, N_EXP // N_GROUP), 2)[0], axis=-1)
    _, gidx = lax.top_k(grp_score, TOPK_GROUP)
    gsel = jnp.any(gidx[..., None] == jnp.arange(N_GROUP), axis=-2)
    emask = jnp.repeat(gsel, N_EXP // N_GROUP, axis=-1)
    _, eidx = lax.top_k(jnp.where(emask, choice, -jnp.inf), TOP_K)
    w = jnp.take_along_axis(s, eidx, axis=-1)
    w = w / jnp.sum(w, axis=-1, keepdims=True) * ROUTED_SCALE
    gates = jnp.sum(jax.nn.one_hot(eidx, N_EXP, dtype=F32) * w[..., None], axis=1)

    def body(acc, xs):
        e, g_e = xs
        y = swiglu(xf, w_exp_in[layer, e], w_exp_out[layer, e])
        return acc + g_e[:, None] * y, None

    acc0 = swiglu(xf, w_sh_in[layer], w_sh_out[layer])
    out, _ = lax.scan(body, acc0, (jnp.arange(N_EXP), gates.T.astype(xf.dtype)))
    return out


def back(x, o_a, o_b, o_m, gate_raw, layer, w_proj_a, w_proj_b, w_proj_m, w_o, ln1_g, ln1_b,
         w_router, router_bias, w_exp_in, w_exp_out, w_sh_in, w_sh_out, ln2_g, ln2_b):
    B, T, D = x.shape
    g = jax.nn.sigmoid(gate_raw.astype(F32)).astype(x.dtype).reshape(B, T, N_BRANCH, D)
    h = (g[:, :, 0] * jnp.einsum('btw,wd->btd', o_a.reshape(B, T, W_A), w_proj_a[layer])
         + g[:, :, 1] * jnp.einsum('btw,wd->btd', o_b, w_proj_b[layer])
         + g[:, :, 2] * jnp.einsum('btw,wd->btd', o_m.reshape(B, T, W_M), w_proj_m[layer]))
    mix = jnp.einsum('btd,de->bte', h, w_o[layer])
    x1 = layer_norm(DN_ALPHA * x + mix, ln1_g[layer], ln1_b[layer])
    ff = moe(x1.reshape(B * T, D), layer, w_router, router_bias, w_exp_in, w_exp_out, w_sh_in, w_sh_out)
    return layer_norm(DN_ALPHA * x1 + ff.reshape(B, T, D), ln2_g[layer], ln2_b[layer])


def setup_inputs(seed: int = 0) -> dict:
    key = jax.random.key(seed)
    ks = list(jax.random.split(key, 40))
    ctr = [0]

    def nk():
        ctr[0] += 1
        return ks[ctr[0] - 1]

    def nrm(shape, scale):
        return jax.random.normal(nk(), shape, F32) * scale

    D = D_MODEL
    n_pages = PAST_LEN // PAGE_SIZE
    n_used = DEC_BATCH * n_pages
    n_phys = n_used + max(1, n_used // 4)
    x_prompt = nrm((BATCH, SEQ, D), 1.0)
    mem_prompt = nrm((BATCH, N_MEM, D), 1.0)
    x_sample = nrm((DEC_BATCH, DEC_SEQ, D), 1.0)
    cache_k = nrm((DEPTH, n_phys, PAGE_SIZE, H_A, DH_A), 1.0)
    cache_v = nrm((DEPTH, n_phys, PAGE_SIZE, H_A, DH_A), 1.0)
    cache_idx_k = nrm((DEPTH, n_phys, PAGE_SIZE, D_IDX), 1.0)
    state_hgrn = nrm((DEPTH, DEC_BATCH, H_B, DK_B, DV_B), 0.5)
    cache_mem_k = nrm((DEPTH, DEC_BATCH, N_MEM, H_M, DH_M), 1.0)
    cache_mem_v = nrm((DEPTH, DEC_BATCH, N_MEM, H_M, DH_M), 1.0)
    page_table = jax.random.permutation(nk(), n_phys)[:n_used].reshape(DEC_BATCH, n_pages).astype(jnp.int32)
    return {
        'x_prompt': x_prompt, 'mem_prompt': mem_prompt, 'x_sample': x_sample,
        'cache_k': cache_k, 'cache_v': cache_v, 'cache_idx_k': cache_idx_k, 'state_hgrn': state_hgrn,
        'cache_mem_k': cache_mem_k, 'cache_mem_v': cache_mem_v, 'page_table': page_table,
        'w_in': nrm((DEPTH, D, N_IN), D ** -0.5),
        'w_idx_qb': nrm((DEPTH, R_IDX, H_IDX * D_IDX), R_IDX ** -0.5),
        'idx_ln_g': 1.0 + nrm((DEPTH, D_IDX), 0.02),
        'idx_ln_b': nrm((DEPTH, D_IDX), 0.02),
        'hgrn_lb_logits': nrm((DEPTH + 1, H_B * DK_B), 0.5),
        'hgrn_norm_g': 1.0 + nrm((DEPTH, DV_B), 0.02),
        'w_mem_kv': nrm((DEPTH, D, 2 * W_M), D ** -0.5),
        'w_proj_a': nrm((DEPTH, W_A, D), (W_A ** -0.5) * DN_BETA),
        'w_proj_b': nrm((DEPTH, W_B, D), (W_B ** -0.5) * DN_BETA),
        'w_proj_m': nrm((DEPTH, W_M, D), (W_M ** -0.5) * DN_BETA),
        'w_o': nrm((DEPTH, D, D), (D ** -0.5) * DN_BETA),
        'ln1_g': 1.0 + nrm((DEPTH, D), 0.02),
        'ln1_b': nrm((DEPTH, D), 0.02),
        'w_router': nrm((DEPTH, D, N_EXP), D ** -0.5),
        'router_bias': nrm((DEPTH, N_EXP), 0.01),
        'w_exp_in': nrm((DEPTH, N_EXP, D, 2 * F_EXP), D ** -0.5),
        'w_exp_out': nrm((DEPTH, N_EXP, F_EXP, D), (F_EXP ** -0.5) * DN_BETA),
        'w_sh_in': nrm((DEPTH, D, 2 * F_SH), D ** -0.5),
        'w_sh_out': nrm((DEPTH, F_SH, D), (F_SH ** -0.5) * DN_BETA),
        'ln2_g': 1.0 + nrm((DEPTH, D), 0.02),
        'ln2_b': nrm((DEPTH, D), 0.02),
    }


def reference(x_prompt, mem_prompt, x_sample, cache_k, cache_v, cache_idx_k, state_hgrn, cache_mem_k,
              cache_mem_v, page_table, w_in, w_idx_qb, idx_ln_g, idx_ln_b, hgrn_lb_logits, hgrn_norm_g,
              w_mem_kv, w_proj_a, w_proj_b, w_proj_m, w_o, ln1_g, ln1_b, w_router, router_bias,
              w_exp_in, w_exp_out, w_sh_in, w_sh_out, ln2_g, ln2_b):
    pos_p = jnp.arange(x_prompt.shape[1])
    pos_s = PAST_LEN + jnp.arange(x_sample.shape[1])
    hp, hs = x_prompt, x_sample
    kp_l, vp_l, ip_l, sp_l, mkp_l, mvp_l = [], [], [], [], [], []
    ks_l, vs_l, is_l, ss_l = [], [], [], []
    for l in range(DEPTH):
        lb = hgrn_lower_bound(hgrn_lb_logits, l)
        (qa, ka, va, qi, ki, wi, fb, ib, qb, gb, qm, graw) = front(hp, pos_p, w_in[l], w_idx_qb[l], idx_ln_g[l], idx_ln_b[l])
        o_a = dsa_prompt(qa, ka, va, qi, wi, ki)
        s0 = jnp.zeros((hp.shape[0], H_B, DK_B, DV_B), F32)
        o_b, s_p = hgrn2(fb, ib, qb, gb, s0, lb, hgrn_norm_g[l])
        mk, mv = mem_kv(mem_prompt, w_mem_kv[l])
        o_m = mem_attend(qm, mk, mv)
        new_hp = back(hp, o_a, o_b, o_m, graw, l, w_proj_a, w_proj_b, w_proj_m, w_o, ln1_g, ln1_b,
                      w_router, router_bias, w_exp_in, w_exp_out, w_sh_in, w_sh_out, ln2_g, ln2_b)
        kp_l.append(ka)
        vp_l.append(va)
        ip_l.append(ki)
        sp_l.append(s_p.astype(hp.dtype))
        mkp_l.append(mk)
        mvp_l.append(mv)
        hp = new_hp
        (qa, ka, va, qi, ki, wi, fb, ib, qb, gb, qm, graw) = front(hs, pos_s, w_in[l], w_idx_qb[l], idx_ln_g[l], idx_ln_b[l])
        o_a = dsa_sample(qa, ka, va, qi, wi, ki, cache_k, cache_v, cache_idx_k, page_table, l)
        o_b, s_s = hgrn2(fb, ib, qb, gb, state_hgrn[l], lb, hgrn_norm_g[l])
        o_m = mem_attend(qm, cache_mem_k[l], cache_mem_v[l])
        new_hs = back(hs, o_a, o_b, o_m, graw, l, w_proj_a, w_proj_b, w_proj_m, w_o, ln1_g, ln1_b,
                      w_router, router_bias, w_exp_in, w_exp_out, w_sh_in, w_sh_out, ln2_g, ln2_b)
        ks_l.append(ka)
        vs_l.append(va)
        is_l.append(ki)
        ss_l.append(s_s.astype(state_hgrn.dtype))
        hs = new_hs
    new_k_prompt = jnp.stack(kp_l)
    new_v_prompt = jnp.stack(vp_l)
    new_idx_k_prompt = jnp.stack(ip_l)
    new_state_hgrn_prompt = jnp.stack(sp_l)
    new_mem_k_prompt = jnp.stack(mkp_l)
    new_mem_v_prompt = jnp.stack(mvp_l)
    new_k_sample = jnp.stack(ks_l)
    new_v_sample = jnp.stack(vs_l)
    new_idx_k_sample = jnp.stack(is_l)
    new_state_hgrn_sample = jnp.stack(ss_l)
    return (hp, hs, new_k_prompt, new_v_prompt, new_idx_k_prompt, new_state_hgrn_prompt,
            new_mem_k_prompt, new_mem_v_prompt, new_k_sample, new_v_sample, new_idx_k_sample,
            new_state_hgrn_sample)
```

```python
import functools
import math

import jax
import jax.numpy as jnp
import numpy as np
from jax import lax
from jax.experimental import pallas as pl
from jax.experimental.pallas import tpu as pltpu

F32 = jnp.float32
BF16 = jnp.bfloat16
I32 = jnp.int32

H_A = 16
DH_A = 128
W_A = H_A * DH_A
H_IDX = 32
D_IDX = 128
R_IDX = 512
IDX_ROPE = 64
IDX_W_SCALE = (H_IDX ** -0.5) * (D_IDX ** -0.5)
TOPK_MAX = 256
H_B = 8
DK_B = 128
DV_B = 128
W_B = H_B * DV_B
H_M = 4
DH_M = 256
W_M = H_M * DH_M
N_EXP = 64
TOP_K = 8
N_GROUP = 8
TOPK_GROUP = 4
ROUTED_SCALE = 2.5
ROPE_THETA = 10000.0
LN_EPS = 1e-5
N_BRANCH = 3
DEPTH = 1
DN_ALPHA = (2.0 * DEPTH) ** 0.25

LANE = 128
NEG = -1e30
VMEM_LIMIT = 56 * 1024 * 1024

C_Q = 0
C_K = W_A
C_V = 2 * W_A
C_CIDX = 3 * W_A
C_KIDX = C_CIDX + R_IDX
C_WIDX = C_KIDX + D_IDX
C_FB = C_WIDX + LANE
C_IB = C_FB + H_B * DK_B
C_QB = C_IB + W_B
C_GB = C_QB + H_B * DK_B
C_QM = C_GB + W_B
C_GATE = C_QM + W_M


def _cparams(sem, vmem=VMEM_LIMIT):
    return pltpu.CompilerParams(dimension_semantics=sem, vmem_limit_bytes=vmem)


def _mm_kernel(x_ref, w_ref, o_ref):
    o_ref[...] = jnp.dot(x_ref[...], w_ref[...].astype(BF16),
                         preferred_element_type=F32).astype(o_ref.dtype)


def matmul(x, w, tm, tn, out_dtype=F32):
    m, k = x.shape
    n = w.shape[1]
    return pl.pallas_call(
        _mm_kernel,
        grid=(m // tm, n // tn),
        in_specs=[pl.BlockSpec((tm, k), lambda i, j: (i, 0)),
                  pl.BlockSpec((k, tn), lambda i, j: (0, j))],
        out_specs=pl.BlockSpec((tm, tn), lambda i, j: (i, j)),
        out_shape=jax.ShapeDtypeStruct((m, n), out_dtype),
        compiler_params=_cparams(("parallel", "parallel")),
        name="matmul",
    )(x, w)


def _rope_tables(pos):
    posf = pos.astype(F32)[:, None]
    half = DH_A // 2
    inv = ROPE_THETA ** (-jnp.arange(half, dtype=F32) / half)
    ang = posf * inv[None, :]
    c, s = jnp.cos(ang), jnp.sin(ang)
    cos_a = jnp.concatenate([c, c], axis=-1)
    sin_a = jnp.concatenate([-s, s], axis=-1)
    half_i = IDX_ROPE // 2
    inv_i = ROPE_THETA ** (-jnp.arange(half_i, dtype=F32) / half_i)
    ang_i = posf * inv_i[None, :]
    ci, si = jnp.cos(ang_i), jnp.sin(ang_i)
    one = jnp.ones((pos.shape[0], D_IDX - IDX_ROPE), F32)
    zero_h = jnp.zeros_like(si)
    zero_r = jnp.zeros_like(one)
    cos_i = jnp.concatenate([ci, ci, one], axis=-1)
    sin_dn = jnp.concatenate([zero_h, si, zero_r], axis=-1)
    sin_up = jnp.concatenate([-si, zero_h, zero_r], axis=-1)
    return cos_a, sin_a, cos_i, sin_dn, sin_up


def _rope_full(x, cos, sin):
    return x * cos + pltpu.roll(x, DH_A // 2, 1) * sin


def _rope_idx(x, cos, sin_dn, sin_up):
    h = IDX_ROPE // 2
    return x * cos + pltpu.roll(x, h, 1) * sin_dn + pltpu.roll(x, D_IDX - h, 1) * sin_up


def _prep_kernel(q_ref, k_ref, ki_ref, cos_ref, sin_ref, cosi_ref, sdn_ref, sup_ref, g_ref, b_ref,
                 qo_ref, ko_ref, kio_ref):
    cos = cos_ref[...]
    sin = sin_ref[...]
    for h in range(H_A):
        sl = slice(h * DH_A, (h + 1) * DH_A)
        qo_ref[:, sl] = _rope_full(q_ref[:, sl], cos, sin).astype(qo_ref.dtype)
        ko_ref[:, sl] = _rope_full(k_ref[:, sl], cos, sin)
    x = ki_ref[...]
    mu = jnp.mean(x, axis=-1, keepdims=True)
    var = jnp.mean(jnp.square(x - mu), axis=-1, keepdims=True)
    xn = (x - mu) * lax.rsqrt(var + LN_EPS) * g_ref[...] + b_ref[...]
    kio_ref[...] = _rope_idx(xn, cosi_ref[...], sdn_ref[...], sup_ref[...])


def prep_qk(z, tables, ln_g, ln_b, tr):
    rows = z.shape[0]
    cos_a, sin_a, cos_i, sin_dn, sin_up = tables
    row_blk = lambda w, c: pl.BlockSpec((tr, w), lambda i: (i, c))
    tab = pl.BlockSpec((tr, LANE), lambda i: (i, 0))
    vec = pl.BlockSpec((1, LANE), lambda i: (0, 0))
    return pl.pallas_call(
        _prep_kernel,
        grid=(rows // tr,),
        in_specs=[row_blk(W_A, C_Q // W_A), row_blk(W_A, C_K // W_A), row_blk(D_IDX, C_KIDX // D_IDX),
                  tab, tab, tab, tab, tab, vec, vec],
        out_specs=[pl.BlockSpec((tr, W_A), lambda i: (i, 0)),
                   pl.BlockSpec((tr, W_A), lambda i: (i, 0)),
                   pl.BlockSpec((tr, D_IDX), lambda i: (i, 0))],
        out_shape=[jax.ShapeDtypeStruct((rows, W_A), BF16),
                   jax.ShapeDtypeStruct((rows, W_A), F32),
                   jax.ShapeDtypeStruct((rows, D_IDX), F32)],
        compiler_params=_cparams(("parallel",)),
        name="prep_qk",
    )(z, z, z, cos_a, sin_a, cos_i, sin_dn, sin_up, ln_g.reshape(1, D_IDX), ln_b.reshape(1, D_IDX))


def _qidx_kernel(c_ref, w_ref, cosi_ref, sdn_ref, sup_ref, o_ref):
    acc = jnp.dot(c_ref[...].astype(BF16), w_ref[...], preferred_element_type=F32)
    cos, sdn, sup = cosi_ref[...], sdn_ref[...], sup_ref[...]
    for h in range(H_IDX):
        o_ref[h] = _rope_idx(acc[:, h * D_IDX:(h + 1) * D_IDX], cos, sdn, sup).astype(o_ref.dtype)


def qidx_proj(z, w_qb, tables, tr):
    rows = z.shape[0]
    _, _, cos_i, sin_dn, sin_up = tables
    tab = pl.BlockSpec((tr, LANE), lambda i: (i, 0))
    return pl.pallas_call(
        _qidx_kernel,
        grid=(rows // tr,),
        in_specs=[pl.BlockSpec((tr, R_IDX), lambda i: (i, C_CIDX // R_IDX)),
                  pl.BlockSpec((R_IDX, H_IDX * D_IDX), lambda i: (0, 0)),
                  tab, tab, tab],
        out_specs=pl.BlockSpec((H_IDX, tr, D_IDX), lambda i: (0, i, 0)),
        out_shape=jax.ShapeDtypeStruct((H_IDX, rows, D_IDX), BF16),
        compiler_params=_cparams(("parallel",)),
        name="qidx_proj",
    )(z, w_qb, cos_i, sin_dn, sin_up)


def _idx_score_body(q_ref, w_ref, k, o_ref):
    tq = q_ref.shape[1]
    q2 = q_ref[...].reshape(H_IDX * tq, D_IDX)
    s = lax.dot_general(q2, k, (((1,), (1,)), ((), ())), preferred_element_type=F32)
    w = w_ref[...] * IDX_W_SCALE
    acc = jnp.zeros((tq, k.shape[0]), F32)
    for h in range(H_IDX):
        acc = acc + jnp.maximum(s[h * tq:(h + 1) * tq], 0.0) * w[:, h:h + 1]
    o_ref[...] = acc


def _idx_score_prompt_kernel(q_ref, w_ref, k_ref, o_ref, *, tq, kc):
    i, j = pl.program_id(0), pl.program_id(1)

    @pl.when(j * kc <= i * tq + tq - 1)
    def _():
        _idx_score_body(q_ref, w_ref, k_ref[...], o_ref)

    @pl.when(j * kc > i * tq + tq - 1)
    def _():
        o_ref[...] = jnp.full(o_ref.shape, NEG, F32)


def idx_scores_prompt(q_idx, z, k_idx_b, n_rows, tq, kc):
    L = k_idx_b.shape[0]

    def kmap(i, j):
        return (jnp.minimum(j, (i * tq + tq - 1) // kc), 0)

    return pl.pallas_call(
        functools.partial(_idx_score_prompt_kernel, tq=tq, kc=kc),
        grid=(n_rows // tq, L // kc),
        in_specs=[pl.BlockSpec((H_IDX, tq, D_IDX), lambda i, j: (0, i, 0)),
                  pl.BlockSpec((tq, LANE), lambda i, j: (i, C_WIDX // LANE)),
                  pl.BlockSpec((kc, D_IDX), kmap)],
        out_specs=pl.BlockSpec((tq, kc), lambda i, j: (i, j)),
        out_shape=jax.ShapeDtypeStruct((n_rows, L), F32),
        compiler_params=_cparams(("parallel", "arbitrary")),
        name="idx_scores_prompt",
    )(q_idx, z, k_idx_b)


def _idx_score_sample_kernel(pt_ref, q_ref, w_ref, *refs, n_pages, n_new):
    page_refs = refs[:n_pages]
    knew_ref = refs[n_pages]
    o_ref = refs[n_pages + 1]
    q = q_ref[...]
    w = w_ref[...] * IDX_W_SCALE
    page = page_refs[0].shape[0]

    def score(k):
        s = lax.dot_general(q, k, (((1,), (1,)), ((), ())), preferred_element_type=F32)
        s = jnp.maximum(s, 0.0) * w
        return jnp.sum(s.reshape(n_new, H_IDX, k.shape[0]), axis=1)

    for p in range(n_pages):
        o_ref[:, p * page:(p + 1) * page] = score(page_refs[p][...].astype(BF16))
    o_ref[:, n_pages * page:] = score(knew_ref[...])


def idx_scores_sample(q_s, w_s, cache_idx, knew_idx, page_table):
    nb, rows, _ = q_s.shape
    n_new = rows // H_IDX
    n_pages = page_table.shape[1]
    page = cache_idx.shape[1]
    page_specs = [pl.BlockSpec((None, page, D_IDX), functools.partial(lambda b, pt, p: (pt[b, p], 0, 0), p=p))
                  for p in range(n_pages)]
    grid_spec = pltpu.PrefetchScalarGridSpec(
        num_scalar_prefetch=1,
        grid=(nb,),
        in_specs=[pl.BlockSpec((None, rows, D_IDX), lambda b, pt: (b, 0, 0)),
                  pl.BlockSpec((None, rows, 1), lambda b, pt: (b, 0, 0))] + page_specs +
                 [pl.BlockSpec((None, page, D_IDX), lambda b, pt: (b, 0, 0))],
        out_specs=pl.BlockSpec((None, n_new, (n_pages + 1) * page), lambda b, pt: (b, 0, 0)),
    )
    return pl.pallas_call(
        functools.partial(_idx_score_sample_kernel, n_pages=n_pages, n_new=n_new),
        grid_spec=grid_spec,
        out_shape=jax.ShapeDtypeStruct((nb, n_new, (n_pages + 1) * page), F32),
        compiler_params=_cparams(("arbitrary",)),
        name="idx_scores_sample",
    )(page_table, q_s, w_s, *([cache_idx] * n_pages), knew_idx)


def _key_to_float(key):
    bits = key ^ (jnp.right_shift(key, 31) & jnp.int32(0x7FFFFFFF))
    return lax.bitcast_convert_type(bits, F32)


def _select_kernel(nch_ref, lim_ref, s_ref, o_ref, m_ref, *, n_sel, kc):
    tr, width = s_ref.shape
    n_chunks = nch_ref[pl.program_id(0)]
    lim = lim_ref[...]
    kq = jnp.minimum(lim + 1, n_sel).astype(F32)

    def mask_chunk(c, carry):
        off = pl.multiple_of(c * kc, kc)
        col = off + lax.broadcasted_iota(I32, (tr, kc), 1)
        m_ref[:, pl.ds(off, kc)] = jnp.where(col <= lim, s_ref[:, pl.ds(off, kc)], -jnp.inf)
        return carry

    lax.fori_loop(0, n_chunks, mask_chunk, 0)

    def count_ge(thr):
        def body(c, acc):
            off = pl.multiple_of(c * kc, kc)
            blk = m_ref[:, pl.ds(off, kc)]
            for u in range(kc // LANE):
                acc = acc + jnp.where(blk[:, u * LANE:(u + 1) * LANE] >= thr, 1.0, 0.0)
            return acc

        acc = lax.fori_loop(0, n_chunks, body, jnp.zeros((tr, LANE), F32))
        return jnp.sum(acc, axis=-1, keepdims=True)

    def bit_step(it, key):
        cand = key + jnp.left_shift(jnp.int32(1), 31 - it)
        ok = count_ge(_key_to_float(cand)) >= kq
        return jnp.where(ok, cand, key)

    key = lax.fori_loop(0, 32, bit_step, jnp.full((tr, 1), jnp.iinfo(jnp.int32).min, I32))
    thr = _key_to_float(key)

    def out_chunk(c, carry):
        off = pl.multiple_of(c * kc, kc)
        o_ref[:, pl.ds(off, kc)] = jnp.where(m_ref[:, pl.ds(off, kc)] >= thr, 0.0, NEG)
        return carry

    lax.fori_loop(0, n_chunks, out_chunk, 0)

    def fill_chunk(c, carry):
        off = pl.multiple_of(c * kc, kc)
        o_ref[:, pl.ds(off, kc)] = jnp.full((tr, kc), NEG, F32)
        return carry

    lax.fori_loop(n_chunks, width // kc, fill_chunk, 0)


def select_bias(scores, lim, n_sel, tr, kc):
    rows, width = scores.shape
    n_tiles = rows // tr
    nch = (jnp.max(lim.reshape(n_tiles, tr), axis=1) + kc) // kc
    grid_spec = pltpu.PrefetchScalarGridSpec(
        num_scalar_prefetch=1,
        grid=(n_tiles,),
        in_specs=[pl.BlockSpec((tr, 1), lambda i, n: (i, 0)),
                  pl.BlockSpec((tr, width), lambda i, n: (i, 0))],
        out_specs=pl.BlockSpec((tr, width), lambda i, n: (i, 0)),
        scratch_shapes=[pltpu.VMEM((tr, width), F32)],
    )
    return pl.pallas_call(
        functools.partial(_select_kernel, n_sel=n_sel, kc=kc),
        grid_spec=grid_spec,
        out_shape=jax.ShapeDtypeStruct((rows, width), F32),
        compiler_params=_cparams(("arbitrary",)),
        name="select_bias",
    )(nch.astype(I32), lim.reshape(rows, 1).astype(I32), scores)


def _attn_prompt_kernel(q_ref, k_ref, v_ref, b_ref, o_ref, m_sc, l_sc, acc_sc, *, tq, tk):
    i, j = pl.program_id(0), pl.program_id(1)
    last = (i * tq + tq - 1) // tk
    scale = DH_A ** -0.5

    @pl.when(j == 0)
    def _():
        m_sc[...] = jnp.full(m_sc.shape, NEG, F32)
        l_sc[...] = jnp.zeros(l_sc.shape, F32)
        acc_sc[...] = jnp.zeros(acc_sc.shape, F32)

    @pl.when(j <= last)
    def _():
        bias = b_ref[...]
        for h in range(H_A):
            sl = slice(h * DH_A, (h + 1) * DH_A)
            s = lax.dot_general(q_ref[:, sl], k_ref[:, sl], (((1,), (1,)), ((), ())),
                                preferred_element_type=F32) * scale + bias
            m_prev = m_sc[h]
            m_new = jnp.maximum(m_prev, jnp.max(s, axis=-1, keepdims=True))
            alpha = jnp.exp(m_prev - m_new)
            p = jnp.exp(s - m_new)
            l_sc[h] = alpha * l_sc[h] + jnp.sum(p, axis=-1, keepdims=True)
            acc_sc[:, sl] = alpha * acc_sc[:, sl] + jnp.dot(p.astype(BF16), v_ref[:, sl],
                                                            preferred_element_type=F32)
            m_sc[h] = m_new

    @pl.when(j == last)
    def _():
        for h in range(H_A):
            sl = slice(h * DH_A, (h + 1) * DH_A)
            o_ref[:, sl] = (acc_sc[:, sl] / l_sc[h]).astype(o_ref.dtype)


def attn_prompt(q, k, v, bias, tq, tk):
    L = q.shape[0]

    def kvmap(i, j):
        return (jnp.minimum(j, (i * tq + tq - 1) // tk), 0)

    def bmap(i, j):
        return (i, jnp.minimum(j, (i * tq + tq - 1) // tk))

    return pl.pallas_call(
        functools.partial(_attn_prompt_kernel, tq=tq, tk=tk),
        grid=(L // tq, L // tk),
        in_specs=[pl.BlockSpec((tq, W_A), lambda i, j: (i, 0)),
                  pl.BlockSpec((tk, W_A), kvmap),
                  pl.BlockSpec((tk, W_A), kvmap),
                  pl.BlockSpec((tq, tk), bmap)],
        out_specs=pl.BlockSpec((tq, W_A), lambda i, j: (i, 0)),
        out_shape=jax.ShapeDtypeStruct((L, W_A), BF16),
        scratch_shapes=[pltpu.VMEM((H_A, tq, 1), F32), pltpu.VMEM((H_A, tq, 1), F32),
                        pltpu.VMEM((tq, W_A), F32)],
        compiler_params=_cparams(("parallel", "arbitrary")),
        name="attn_prompt",
    )(q, k, v, bias)


def _attn_sample_kernel(pt_ref, q_ref, kc_ref, vc_ref, kn_ref, vn_ref, b_ref, o_ref, m_sc, l_sc, acc_sc,
                        *, n_pages, n_new):
    p = pl.program_id(1)
    scale = DH_A ** -0.5

    @pl.when(p == 0)
    def _():
        m_sc[...] = jnp.full(m_sc.shape, NEG, F32)
        l_sc[...] = jnp.zeros(l_sc.shape, F32)
        acc_sc[...] = jnp.zeros(acc_sc.shape, F32)

    def update(k, v):
        s = lax.dot_general(q_ref[...], k, (((1,), (1,)), ((), ())),
                            preferred_element_type=F32) * scale + b_ref[...]
        m_prev = m_sc[...]
        m_new = jnp.maximum(m_prev, jnp.max(s, axis=-1, keepdims=True))
        alpha = jnp.exp(m_prev - m_new)
        pr = jnp.exp(s - m_new)
        l_sc[...] = alpha * l_sc[...] + jnp.sum(pr, axis=-1, keepdims=True)
        acc_sc[...] = alpha * acc_sc[...] + jnp.dot(pr.astype(BF16), v, preferred_element_type=F32)
        m_sc[...] = m_new

    @pl.when(p < n_pages)
    def _():
        update(kc_ref[...].astype(BF16), vc_ref[...].astype(BF16))

    @pl.when(p == n_pages)
    def _():
        update(kn_ref[...], vn_ref[...])
        for h in range(H_A):
            rs = slice(h * n_new, (h + 1) * n_new)
            cs = slice(h * DH_A, (h + 1) * DH_A)
            o_ref[:, cs] = acc_sc[rs, cs] / l_sc[rs, :]


def attn_sample(q_bd, cache_k, cache_v, knew, vnew, bias_x, page_table, n_new):
    nb, rows, _ = q_bd.shape
    n_pages = page_table.shape[1]
    page = cache_k.shape[1]

    def cmap(b, p, pt):
        return (pt[b, jnp.minimum(p, n_pages - 1)], 0, 0)

    grid_spec = pltpu.PrefetchScalarGridSpec(
        num_scalar_prefetch=1,
        grid=(nb, n_pages + 1),
        in_specs=[pl.BlockSpec((None, rows, W_A), lambda b, p, pt: (b, 0, 0)),
                  pl.BlockSpec((None, page, W_A), cmap),
                  pl.BlockSpec((None, page, W_A), cmap),
                  pl.BlockSpec((None, page, W_A), lambda b, p, pt: (b, 0, 0)),
                  pl.BlockSpec((None, page, W_A), lambda b, p, pt: (b, 0, 0)),
                  pl.BlockSpec((None, rows, page), lambda b, p, pt: (b, 0, p))],
        out_specs=pl.BlockSpec((None, n_new, W_A), lambda b, p, pt: (b, 0, 0)),
        scratch_shapes=[pltpu.VMEM((rows, 1), F32), pltpu.VMEM((rows, 1), F32), pltpu.VMEM((rows, W_A), F32)],
    )
    return pl.pallas_call(
        functools.partial(_attn_sample_kernel, n_pages=n_pages, n_new=n_new),
        grid_spec=grid_spec,
        out_shape=jax.ShapeDtypeStruct((nb, n_new, W_A), F32),
        compiler_params=_cparams(("parallel", "arbitrary")),
        name="attn_sample",
    )(page_table, q_bd, cache_k, cache_v, knew, vnew, bias_x)


def _hgrn_chunk(fr, ic, qc, lb, st, sub, row_valid=None):
    C = fr.shape[0]
    lf = jnp.log(lb + (1.0 - lb) * jax.nn.sigmoid(fr))
    kk = (1.0 - lb) * jax.nn.sigmoid(-fr)
    if row_valid is not None:
        lf = jnp.where(row_valid, lf, 0.0)
        kk = jnp.where(row_valid, kk, 0.0)
    r = lax.broadcasted_iota(I32, (C, C), 0)
    c = lax.broadcasted_iota(I32, (C, C), 1)
    tri = jnp.where(r >= c, 1.0, 0.0).astype(F32)
    G = jnp.dot(tri, lf, preferred_element_type=F32, precision=lax.Precision.HIGHEST)
    nt = (((1,), (1,)), ((), ()))
    o_inter = lax.dot_general((qc * jnp.exp(G)).astype(BF16), st.astype(BF16), nt, preferred_element_type=F32)
    ic_b = ic.astype(BF16)
    t3 = lax.broadcasted_iota(I32, (sub, sub, DK_B), 0)
    s3 = lax.broadcasted_iota(I32, (sub, sub, DK_B), 1)
    pieces = []
    for a in range(C // sub):
        r0 = a * sub
        Gs, ks, qs, is_ = G[r0:r0 + sub], kk[r0:r0 + sub], qc[r0:r0 + sub], ic[r0:r0 + sub]
        dec = jnp.exp(jnp.where(t3 >= s3, Gs[:, None, :] - Gs[None, :, :], -jnp.inf))
        a3 = jnp.sum(dec * ks[None, :, :] * qs[:, None, :], axis=-1, keepdims=True)
        o_a = jnp.sum(a3 * is_[None, :, :], axis=1)
        if a > 0:
            gb = G[r0 - 1:r0]
            qa = (qs * jnp.exp(Gs - gb)).astype(BF16)
            kb = (kk[:r0] * jnp.exp(gb - G[:r0])).astype(BF16)
            A = lax.dot_general(qa, kb, nt, preferred_element_type=F32)
            o_a = o_a + jnp.dot(A.astype(BF16), ic_b[:r0], preferred_element_type=F32)
        pieces.append(o_a)
    o = o_inter + (pieces[0] if len(pieces) == 1 else jnp.concatenate(pieces, axis=0))
    g_last = G[C - 1:C]
    kd = (kk * jnp.exp(g_last - G)).astype(BF16)
    st_new = st * jnp.exp(g_last) + lax.dot_general(ic_b, kd, (((0,), (0,)), ((), ())),
                                                    preferred_element_type=F32)
    return o, st_new


def _hgrn_finish(o, gate, norm_g):
    o = o * lax.rsqrt(jnp.mean(jnp.square(o), axis=-1, keepdims=True) + LN_EPS)
    return o * norm_g * (gate * jax.nn.sigmoid(gate))


def _hgrn_prompt_kernel(f_ref, i_ref, q_ref, g_ref, lb_ref, ng_ref, o_ref, s_ref, st_sc, *, chunk, sub):
    j = pl.program_id(1)

    @pl.when(j == 0)
    def _():
        st_sc[...] = jnp.zeros(st_sc.shape, F32)

    lb = lb_ref[...]
    ng = ng_ref[...]

    def body(c, carry):
        rows = pl.ds(pl.multiple_of(c * chunk, chunk), chunk)
        o, st_new = _hgrn_chunk(f_ref[rows, :], i_ref[rows, :], q_ref[rows, :], lb, st_sc[...], sub)
        st_sc[...] = st_new
        o_ref[rows, :] = _hgrn_finish(o, g_ref[rows, :], ng).astype(o_ref.dtype)
        return carry

    lax.fori_loop(0, f_ref.shape[0] // chunk, body, 0)

    @pl.when(j == pl.num_programs(1) - 1)
    def _():
        s_ref[...] = st_sc[...].T


def hgrn_prompt(z, n_rows, lb, norm_g, tb, chunk, sub):
    blk = lambda c0: pl.BlockSpec((tb, DK_B), lambda h, j, c0=c0: (j, c0 // DK_B + h))
    vec = pl.BlockSpec((1, DK_B), lambda h, j: (0, h))
    return pl.pallas_call(
        functools.partial(_hgrn_prompt_kernel, chunk=chunk, sub=sub),
        grid=(H_B, n_rows // tb),
        in_specs=[blk(C_FB), blk(C_IB), blk(C_QB), blk(C_GB), vec,
                  pl.BlockSpec((1, DV_B), lambda h, j: (0, 0))],
        out_specs=[pl.BlockSpec((tb, DV_B), lambda h, j: (j, h)),
                   pl.BlockSpec((None, DK_B, DV_B), lambda h, j: (h, 0, 0))],
        out_shape=[jax.ShapeDtypeStruct((n_rows, W_B), BF16),
                   jax.ShapeDtypeStruct((H_B, DK_B, DV_B), F32)],
        scratch_shapes=[pltpu.VMEM((DV_B, DK_B), F32)],
        compiler_params=_cparams(("parallel", "arbitrary")),
        name="hgrn_prompt",
    )(z, z, z, z, lb.reshape(1, H_B * DK_B), norm_g.reshape(1, DV_B))


def _hgrn_sample_kernel(x_ref, s_ref, lb_ref, ng_ref, o_ref, so_ref, *, n_new, seqs):
    rows = seqs * n_new
    rid = lax.broadcasted_iota(I32, (rows, DK_B), 0)
    ng = ng_ref[...]
    for h in range(H_B):
        col = lambda g: slice(g * W_B + h * DK_B, g * W_B + (h + 1) * DK_B)
        fr, ic, qc, gate = x_ref[:, col(0)], x_ref[:, col(1)], x_ref[:, col(2)], x_ref[:, col(3)]
        lb = lb_ref[:, h * DK_B:(h + 1) * DK_B]
        o = jnp.zeros((rows, DV_B), F32)
        for b in range(seqs):
            valid = (rid >= b * n_new) & (rid < (b + 1) * n_new)
            ob, st_new = _hgrn_chunk(fr, ic, qc, lb, s_ref[b, h].T, rows, row_valid=valid)
            so_ref[b, h] = st_new.T
            o = jnp.where(valid, ob, o)
        o_ref[:, h * DV_B:(h + 1) * DV_B] = _hgrn_finish(o, gate, ng).astype(o_ref.dtype)


def hgrn_sample(xs, state, lb, norm_g, n_new, seqs):
    nb = state.shape[0]
    rows = seqs * n_new
    return pl.pallas_call(
        functools.partial(_hgrn_sample_kernel, n_new=n_new, seqs=seqs),
        grid=(nb // seqs,),
        in_specs=[pl.BlockSpec((rows, 4 * W_B), lambda i: (i, 0)),
                  pl.BlockSpec((seqs, H_B, DK_B, DV_B), lambda i: (i, 0, 0, 0)),
                  pl.BlockSpec((1, H_B * DK_B), lambda i: (0, 0)),
                  pl.BlockSpec((1, DV_B), lambda i: (0, 0))],
        out_specs=[pl.BlockSpec((rows, W_B), lambda i: (i, 0)),
                   pl.BlockSpec((seqs, H_B, DK_B, DV_B), lambda i: (i, 0, 0, 0))],
        out_shape=[jax.ShapeDtypeStruct((nb * n_new, W_B), F32),
                   jax.ShapeDtypeStruct(state.shape, F32)],
        compiler_params=_cparams(("parallel",)),
        name="hgrn_sample",
    )(xs, state, lb.reshape(1, H_B * DK_B), norm_g.reshape(1, DV_B))


def _softmax_rows(s):
    m = jnp.max(s, axis=-1, keepdims=True)
    p = jnp.exp(s - m)
    return p / jnp.sum(p, axis=-1, keepdims=True)


def _mem_attn_prompt_kernel(*refs):
    q_refs, mk_ref, mv_ref, o_ref = refs[:H_M], refs[H_M], refs[H_M + 1], refs[H_M + 2]
    nt = (((1,), (1,)), ((), ()))
    for h in range(H_M):
        sl = slice(h * DH_M, (h + 1) * DH_M)
        s = lax.dot_general(q_refs[h][...].astype(BF16), mk_ref[:, sl], nt,
                            preferred_element_type=F32) * (DH_M ** -0.5)
        p = _softmax_rows(s).astype(BF16)
        o_ref[:, sl] = jnp.dot(p, mv_ref[:, sl], preferred_element_type=F32).astype(o_ref.dtype)


def mem_attn_prompt(z, n_rows, mk, mv, tr):
    n_mem = mk.shape[0]
    q_specs = [pl.BlockSpec((tr, DH_M), functools.partial(lambda i, h: (i, C_QM // DH_M + h), h=h))
               for h in range(H_M)]
    kv = pl.BlockSpec((n_mem, W_M), lambda i: (0, 0))
    return pl.pallas_call(
        _mem_attn_prompt_kernel,
        grid=(n_rows // tr,),
        in_specs=q_specs + [kv, kv],
        out_specs=pl.BlockSpec((tr, W_M), lambda i: (i, 0)),
        out_shape=jax.ShapeDtypeStruct((n_rows, W_M), BF16),
        compiler_params=_cparams(("parallel",)),
        name="mem_attn_prompt",
    )(*([z] * H_M), mk, mv)


def _mem_attn_sample_kernel(q_ref, k_ref, v_ref, o_ref, *, n_new):
    nt = (((1,), (1,)), ((), ()))
    s = lax.dot_general(q_ref[...], k_ref[...].astype(BF16), nt,
                        preferred_element_type=F32) * (DH_M ** -0.5)
    p = _softmax_rows(s).astype(BF16)
    full = jnp.dot(p, v_ref[...].astype(BF16), preferred_element_type=F32)
    for h in range(H_M):
        o_ref[:, h * DH_M:(h + 1) * DH_M] = full[h * n_new:(h + 1) * n_new, h * DH_M:(h + 1) * DH_M]


def mem_attn_sample(q_bd, cache_mk, cache_mv, n_new):
    nb, rows, _ = q_bd.shape
    n_mem = cache_mk.shape[1]
    return pl.pallas_call(
        functools.partial(_mem_attn_sample_kernel, n_new=n_new),
        grid=(nb,),
        in_specs=[pl.BlockSpec((None, rows, W_M), lambda b: (b, 0, 0)),
                  pl.BlockSpec((None, n_mem, W_M), lambda b: (b, 0, 0)),
                  pl.BlockSpec((None, n_mem, W_M), lambda b: (b, 0, 0))],
        out_specs=pl.BlockSpec((None, n_new, W_M), lambda b: (b, 0, 0)),
        out_shape=jax.ShapeDtypeStruct((nb, n_new, W_M), F32),
        compiler_params=_cparams(("parallel",)),
        name="mem_attn_sample",
    )(q_bd, cache_mk, cache_mv)


def _gated_proj_kernel(oa_ref, ob_ref, om_ref, wa_ref, wb_ref, wm_ref, g0_ref, g1_ref, g2_ref, o_ref):
    def branch(x_ref, w_ref, g_ref):
        return jax.nn.sigmoid(g_ref[...]) * jnp.dot(x_ref[...], w_ref[...], preferred_element_type=F32)

    h = branch(oa_ref, wa_ref, g0_ref) + branch(ob_ref, wb_ref, g1_ref) + branch(om_ref, wm_ref, g2_ref)
    o_ref[...] = h.astype(o_ref.dtype)


def gated_proj(o_a, o_b, o_m, w_a, w_b, w_m, z, d_model, tm, tn):
    rows = o_a.shape[0]
    x_spec = lambda w: pl.BlockSpec((tm, w), lambda i, j: (i, 0))
    w_spec = lambda w: pl.BlockSpec((w, tn), lambda i, j: (0, j))
    g_spec = lambda g: pl.BlockSpec((tm, tn), lambda i, j, g=g: (i, (C_GATE + g * d_model) // tn + j))
    return pl.pallas_call(
        _gated_proj_kernel,
        grid=(rows // tm, d_model // tn),
        in_specs=[x_spec(W_A), x_spec(W_B), x_spec(W_M), w_spec(W_A), w_spec(W_B), w_spec(W_M),
                  g_spec(0), g_spec(1), g_spec(2)],
        out_specs=pl.BlockSpec((tm, tn), lambda i, j: (i, j)),
        out_shape=jax.ShapeDtypeStruct((rows, d_model), BF16),
        compiler_params=_cparams(("parallel", "parallel")),
        name="gated_proj",
    )(o_a, o_b, o_m, w_a, w_b, w_m, z, z, z)


def _layer_norm(x, g, b):
    mu = jnp.mean(x, axis=-1, keepdims=True)
    var = jnp.mean(jnp.square(x - mu), axis=-1, keepdims=True)
    return (x - mu) * lax.rsqrt(var + LN_EPS) * g + b


def _out_proj_ln_kernel(h_ref, w_ref, x_ref, g_ref, b_ref, o_ref, ob_ref, acc_sc):
    k = pl.program_id(1)

    @pl.when(k == 0)
    def _():
        acc_sc[...] = jnp.zeros(acc_sc.shape, F32)

    acc_sc[...] += jnp.dot(h_ref[...], w_ref[...], preferred_element_type=F32)

    @pl.when(k == pl.num_programs(1) - 1)
    def _():
        y = _layer_norm(DN_ALPHA * x_ref[...] + acc_sc[...], g_ref[...], b_ref[...])
        o_ref[...] = y
        ob_ref[...] = y.astype(ob_ref.dtype)


def out_proj_ln(h, w_o, x, ln_g, ln_b, tm, tk):
    rows, d = x.shape
    vec = pl.BlockSpec((1, d), lambda i, k: (0, 0))
    return pl.pallas_call(
        _out_proj_ln_kernel,
        grid=(rows // tm, d // tk),
        in_specs=[pl.BlockSpec((tm, tk), lambda i, k: (i, k)),
                  pl.BlockSpec((tk, d), lambda i, k: (k, 0)),
                  pl.BlockSpec((tm, d), lambda i, k: (i, 0)), vec, vec],
        out_specs=[pl.BlockSpec((tm, d), lambda i, k: (i, 0)),
                   pl.BlockSpec((tm, d), lambda i, k: (i, 0))],
        out_shape=[jax.ShapeDtypeStruct((rows, d), F32), jax.ShapeDtypeStruct((rows, d), BF16)],
        scratch_shapes=[pltpu.VMEM((tm, d), F32)],
        compiler_params=_cparams(("parallel", "arbitrary")),
        name="out_proj_ln",
    )(h, w_o, x, ln_g.reshape(1, d), ln_b.reshape(1, d))


def _router_kernel(x_ref, w_ref, b_ref, sel_ref, exp_ref, idx_ref, wgt_ref, rank_ref, cnt_ref, carry_sc):
    i = pl.program_id(0)

    @pl.when(i == 0)
    def _():
        carry_sc[...] = jnp.zeros(carry_sc.shape, F32)

    tr = x_ref.shape[0]
    per_group = N_EXP // N_GROUP
    lane = lax.broadcasted_iota(I32, (tr, LANE), 1)
    s = jax.nn.sigmoid(jnp.dot(x_ref[...], w_ref[...], preferred_element_type=F32))
    real = lane < N_EXP
    choice = jnp.where(real, s + b_ref[...], 0.0)
    hi = lax.Precision.HIGHEST

    def first_argmax(v):
        mx = jnp.max(v, axis=-1, keepdims=True)
        return mx, jnp.min(jnp.where(v == mx, lane, LANE), axis=-1, keepdims=True)

    members = [jnp.dot(choice, sel_ref[r], preferred_element_type=F32, precision=hi) for r in range(per_group)]
    m1 = functools.reduce(jnp.maximum, members)
    first = functools.reduce(jnp.minimum, [jnp.where(members[r] == m1, r, per_group) for r in range(per_group)])
    m2 = functools.reduce(jnp.maximum, [jnp.where(first == r, -jnp.inf, members[r]) for r in range(per_group)])
    cur = jnp.where(lane < N_GROUP, m1 + m2, -jnp.inf)
    gsel = jnp.zeros((tr, LANE), F32)
    for _ in range(TOPK_GROUP):
        _, gi = first_argmax(cur)
        hit = lane == gi
        gsel = jnp.where(hit, 1.0, gsel)
        cur = jnp.where(hit, -jnp.inf, cur)
    emask = jnp.dot(gsel, exp_ref[...], preferred_element_type=F32, precision=hi)
    masked = jnp.where((emask > 0.5) & real, choice, -jnp.inf)

    idx_out = jnp.zeros((tr, LANE), I32)
    wgt_out = jnp.zeros((tr, LANE), F32)
    onehot = jnp.zeros((tr, LANE), F32)
    picks = []
    for k in range(TOP_K):
        _, ei = first_argmax(masked)
        hit = lane == ei
        wk = jnp.sum(jnp.where(hit, s, 0.0), axis=-1, keepdims=True)
        masked = jnp.where(hit, -jnp.inf, masked)
        onehot = jnp.where(hit, 1.0, onehot)
        idx_out = jnp.where(lane == k, ei, idx_out)
        wgt_out = jnp.where(lane == k, wk, wgt_out)
        picks.append(hit)
    wsum = jnp.sum(wgt_out, axis=-1, keepdims=True)
    wgt_ref[...] = wgt_out / wsum * ROUTED_SCALE
    idx_ref[...] = idx_out

    r = lax.broadcasted_iota(I32, (tr, tr), 0)
    c = lax.broadcasted_iota(I32, (tr, tr), 1)
    before = jnp.where(r > c, 1.0, 0.0).astype(BF16)
    rank = carry_sc[...] + jnp.dot(before, onehot.astype(BF16), preferred_element_type=F32)
    rank_out = jnp.zeros((tr, LANE), F32)
    for k in range(TOP_K):
        rk = jnp.sum(jnp.where(picks[k], rank, 0.0), axis=-1, keepdims=True)
        rank_out = jnp.where(lane == k, rk, rank_out)
    rank_ref[...] = rank_out.astype(I32)
    carry_sc[...] += jnp.sum(onehot, axis=0, keepdims=True)
    cnt_ref[...] = carry_sc[...].astype(I32)


def router(x1b, w_router, router_bias, tr):
    rows, d = x1b.shape
    per_group = N_EXP // N_GROUP
    wr = jnp.pad(w_router, ((0, 0), (0, LANE - N_EXP))).astype(BF16)
    bias = jnp.pad(router_bias.astype(F32), (0, LANE - N_EXP)).reshape(1, LANE)
    e = np.arange(LANE)
    sel = np.stack([(e[:, None] == per_group * e[None, :] + r) & (e[None, :] < N_GROUP)
                    for r in range(per_group)]).astype(np.float32)
    expand = ((e[None, :] // per_group == e[:, None]) & (e[None, :] < N_EXP)).astype(np.float32)
    tile = pl.BlockSpec((tr, LANE), lambda i: (i, 0))
    return pl.pallas_call(
        _router_kernel,
        grid=(rows // tr,),
        in_specs=[pl.BlockSpec((tr, d), lambda i: (i, 0)),
                  pl.BlockSpec((d, LANE), lambda i: (0, 0)),
                  pl.BlockSpec((1, LANE), lambda i: (0, 0)),
                  pl.BlockSpec((per_group, LANE, LANE), lambda i: (0, 0, 0)),
                  pl.BlockSpec((LANE, LANE), lambda i: (0, 0))],
        out_specs=[tile, tile, tile, pl.BlockSpec((1, LANE), lambda i: (0, 0))],
        out_shape=[jax.ShapeDtypeStruct((rows, LANE), I32), jax.ShapeDtypeStruct((rows, LANE), F32),
                   jax.ShapeDtypeStruct((rows, LANE), I32), jax.ShapeDtypeStruct((1, LANE), I32)],
        scratch_shapes=[pltpu.VMEM((1, LANE), F32)],
        compiler_params=_cparams(("arbitrary",)),
        name="router",
    )(x1b, wr, bias, jnp.asarray(sel), jnp.asarray(expand))


def _gather_rows_kernel(idx_ref, src_ref, o_ref, buf, sem, *, ts):
    base = pl.program_id(0) * ts

    def row_copy(r):
        return pltpu.make_async_copy(src_ref.at[pl.ds(idx_ref[base + r], 1)], buf.at[pl.ds(r, 1)], sem)

    def issue(r, carry):
        row_copy(r).start()
        return carry

    lax.fori_loop(0, ts, issue, 0)

    def drain(r, carry):
        row_copy(r).wait()
        return carry

    lax.fori_loop(0, ts, drain, 0)
    o_ref[...] = buf[...].astype(o_ref.dtype)


def gather_rows(src, idx, ts, out_dtype):
    d = src.shape[1]
    n_out = idx.shape[0]
    grid_spec = pltpu.PrefetchScalarGridSpec(
        num_scalar_prefetch=1,
        grid=(n_out // ts,),
        in_specs=[pl.BlockSpec(memory_space=pl.ANY)],
        out_specs=pl.BlockSpec((ts, d), lambda t, idx: (t, 0)),
        scratch_shapes=[pltpu.VMEM((ts, d), src.dtype), pltpu.SemaphoreType.DMA(())],
    )
    return pl.pallas_call(
        functools.partial(_gather_rows_kernel, ts=ts),
        grid_spec=grid_spec,
        out_shape=jax.ShapeDtypeStruct((n_out, d), out_dtype),
        compiler_params=_cparams(("arbitrary",)),
        name="gather_rows",
    )(idx, src)


def _expert_up_kernel(te_ref, nt_ref, x_ref, wa_ref, wb_ref, o_ref, wa_sc, wb_sc):
    t = pl.program_id(1)
    changed = jnp.logical_or(t == 0, te_ref[t] != te_ref[jnp.maximum(t - 1, 0)])

    @pl.when(jnp.logical_and(changed, t < nt_ref[0]))
    def _():
        wa_sc[...] = wa_ref[...].astype(BF16)
        wb_sc[...] = wb_ref[...].astype(BF16)

    @pl.when(t < nt_ref[0])
    def _():
        x = x_ref[...]
        a = jnp.dot(x, wa_sc[...], preferred_element_type=F32)
        b = jnp.dot(x, wb_sc[...], preferred_element_type=F32)
        o_ref[...] = (a * jax.nn.sigmoid(a) * b).astype(o_ref.dtype)

    @pl.when(t >= nt_ref[0])
    def _():
        o_ref[...] = jnp.zeros(o_ref.shape, o_ref.dtype)


def expert_up(xs, w_in, tile_expert, n_tiles_used, tm, fc):
    n_slots, d = xs.shape
    f = w_in.shape[2] // 2
    nf = f // fc

    def tmap(fi, t, te, nt):
        return jnp.minimum(t, nt[0] - 1)

    grid_spec = pltpu.PrefetchScalarGridSpec(
        num_scalar_prefetch=2,
        grid=(nf, n_slots // tm),
        in_specs=[pl.BlockSpec((tm, d), lambda fi, t, te, nt: (tmap(fi, t, te, nt), 0)),
                  pl.BlockSpec((None, d, fc), lambda fi, t, te, nt: (te[tmap(fi, t, te, nt)], 0, fi)),
                  pl.BlockSpec((None, d, fc), lambda fi, t, te, nt: (te[tmap(fi, t, te, nt)], 0, nf + fi))],
        out_specs=pl.BlockSpec((tm, fc), lambda fi, t, te, nt: (t, fi)),
        scratch_shapes=[pltpu.VMEM((d, fc), BF16), pltpu.VMEM((d, fc), BF16)],
    )
    return pl.pallas_call(
        _expert_up_kernel,
        grid_spec=grid_spec,
        out_shape=jax.ShapeDtypeStruct((n_slots, f), BF16),
        compiler_params=_cparams(("arbitrary", "arbitrary")),
        name="expert_up",
    )(tile_expert, n_tiles_used, xs, w_in, w_in)


def _expert_down_kernel(te_ref, nt_ref, h_ref, w_ref, g_ref, o_ref, w_sc):
    t = pl.program_id(1)
    changed = jnp.logical_or(t == 0, te_ref[t] != te_ref[jnp.maximum(t - 1, 0)])

    @pl.when(jnp.logical_and(changed, t < nt_ref[0]))
    def _():
        w_sc[...] = w_ref[...].astype(BF16)

    @pl.when(t < nt_ref[0])
    def _():
        o_ref[...] = jnp.dot(h_ref[...], w_sc[...], preferred_element_type=F32) * g_ref[...]

    @pl.when(t >= nt_ref[0])
    def _():
        o_ref[...] = jnp.zeros(o_ref.shape, o_ref.dtype)


def expert_down(hs, w_out, gate_slot, tile_expert, n_tiles_used, tm, nc):
    n_slots, f = hs.shape
    d = w_out.shape[2]

    def tmap(t, nt):
        return jnp.minimum(t, nt[0] - 1)

    grid_spec = pltpu.PrefetchScalarGridSpec(
        num_scalar_prefetch=2,
        grid=(d // nc, n_slots // tm),
        in_specs=[pl.BlockSpec((tm, f), lambda j, t, te, nt: (tmap(t, nt), 0)),
                  pl.BlockSpec((None, f, nc), lambda j, t, te, nt: (te[tmap(t, nt)], 0, j)),
                  pl.BlockSpec((tm, 1), lambda j, t, te, nt: (tmap(t, nt), 0))],
        out_specs=pl.BlockSpec((tm, nc), lambda j, t, te, nt: (t, j)),
        scratch_shapes=[pltpu.VMEM((f, nc), BF16)],
    )
    return pl.pallas_call(
        _expert_down_kernel,
        grid_spec=grid_spec,
        out_shape=jax.ShapeDtypeStruct((n_slots, d), F32),
        compiler_params=_cparams(("arbitrary", "arbitrary")),
        name="expert_down",
    )(tile_expert, n_tiles_used, hs, w_out, gate_slot)


def _swiglu_up_kernel(x_ref, wa_ref, wb_ref, o_ref):
    x = x_ref[...]
    a = jnp.dot(x, wa_ref[...].astype(BF16), preferred_element_type=F32)
    b = jnp.dot(x, wb_ref[...].astype(BF16), preferred_element_type=F32)
    o_ref[...] = (a * jax.nn.sigmoid(a) * b).astype(o_ref.dtype)


def swiglu_up(x, w_in, tm, fc):
    rows, d = x.shape
    f = w_in.shape[1] // 2
    nf = f // fc
    return pl.pallas_call(
        _swiglu_up_kernel,
        grid=(nf, rows // tm),
        in_specs=[pl.BlockSpec((tm, d), lambda fi, i: (i, 0)),
                  pl.BlockSpec((d, fc), lambda fi, i: (0, fi)),
                  pl.BlockSpec((d, fc), lambda fi, i: (0, nf + fi))],
        out_specs=pl.BlockSpec((tm, fc), lambda fi, i: (i, fi)),
        out_shape=jax.ShapeDtypeStruct((rows, f), BF16),
        compiler_params=_cparams(("parallel", "parallel")),
        name="swiglu_up",
    )(x, w_in, w_in)


def _combine_ln_kernel(slot_ref, y_ref, sh_ref, x_ref, g_ref, b_ref, o_ref, buf, sem, *, tt):
    base = pl.program_id(0) * tt * TOP_K

    def row_copy(r):
        n, k = r // TOP_K, r % TOP_K
        return pltpu.make_async_copy(y_ref.at[pl.ds(slot_ref[base + r], 1)], buf.at[pl.ds(k * tt + n, 1)], sem)

    def issue(r, carry):
        row_copy(r).start()
        return carry

    lax.fori_loop(0, tt * TOP_K, issue, 0)

    def drain(r, carry):
        row_copy(r).wait()
        return carry

    lax.fori_loop(0, tt * TOP_K, drain, 0)
    ff = sh_ref[...]
    for k in range(TOP_K):
        ff = ff + buf[k * tt:(k + 1) * tt, :]
    o_ref[...] = _layer_norm(DN_ALPHA * x_ref[...] + ff, g_ref[...], b_ref[...])


def combine_ln(y, slots, shared, x1, ln_g, ln_b, tt):
    rows, d = x1.shape
    vec = pl.BlockSpec((1, d), lambda i, s: (0, 0))
    tile = pl.BlockSpec((tt, d), lambda i, s: (i, 0))
    grid_spec = pltpu.PrefetchScalarGridSpec(
        num_scalar_prefetch=1,
        grid=(rows // tt,),
        in_specs=[pl.BlockSpec(memory_space=pl.ANY), tile, tile, vec, vec],
        out_specs=tile,
        scratch_shapes=[pltpu.VMEM((tt * TOP_K, d), F32), pltpu.SemaphoreType.DMA(())],
    )
    return pl.pallas_call(
        functools.partial(_combine_ln_kernel, tt=tt),
        grid_spec=grid_spec,
        out_shape=jax.ShapeDtypeStruct((rows, d), F32),
        compiler_params=_cparams(("arbitrary",)),
        name="combine_ln",
    )(slots, y, shared, x1, ln_g.reshape(1, d), ln_b.reshape(1, d))


def _largest_tile(n, candidates):
    for c in candidates:
        if n % c == 0:
            return c
    raise ValueError(f"no tile in {candidates} divides {n}")


def moe(x1, x1b, w_router, router_bias, w_exp_in, w_exp_out, w_sh_in, w_sh_out, ln_g, ln_b):
    rows, d = x1.shape
    f = w_exp_out.shape[1]
    tm = 256
    eidx_p, wgt_p, rank_p, cnt = router(x1b, w_router, router_bias, _largest_tile(rows, (256, 128, 64, 32, 16)))
    eidx, wgt, rank = eidx_p[:, :TOP_K], wgt_p[:, :TOP_K], rank_p[:, :TOP_K]
    counts = cnt[0, :N_EXP]
    padded = (counts + tm - 1) // tm * tm
    ends = jnp.cumsum(padded)
    offs = ends - padded
    n_tiles = (rows * TOP_K + N_EXP * (tm - 1) + tm - 1) // tm
    n_slots = n_tiles * tm
    n_used = (ends[-1] // tm).astype(I32).reshape(1)
    tile_expert = jnp.minimum(jnp.searchsorted(ends, jnp.arange(n_tiles, dtype=I32) * tm, side="right"),
                              N_EXP - 1).astype(I32)
    slot = (offs[eidx] + rank).astype(I32).reshape(-1)
    token = jnp.repeat(jnp.arange(rows, dtype=I32), TOP_K)
    token_of_slot = jnp.zeros((n_slots,), I32).at[slot].set(token)
    gate_of_slot = jnp.zeros((n_slots,), F32).at[slot].set(wgt.reshape(-1))

    xs = gather_rows(x1, token_of_slot, tm, BF16)
    fc = _largest_tile(f, (512, 256, 128))
    hs = expert_up(xs, w_exp_in, tile_expert, n_used, tm, fc)
    ys = expert_down(hs, w_exp_out, gate_of_slot.reshape(n_slots, 1), tile_expert, n_used, tm,
                     _largest_tile(d, (2048, 1024, 512, 256)))
    t_sh = _largest_tile(rows, (512, 256, 128, 64, 32, 16))
    h_sh = swiglu_up(x1b, w_sh_in, t_sh, _largest_tile(w_sh_in.shape[1] // 2, (512, 256, 128)))
    y_sh = matmul(h_sh, w_sh_out, t_sh, _largest_tile(d, (512, 256)))
    return combine_ln(ys, slot, y_sh, x1, ln_g, ln_b, _largest_tile(rows, (32, 16, 8)))


def kernel(x_prompt, mem_prompt, x_sample, cache_k, cache_v, cache_idx_k, state_hgrn, cache_mem_k, cache_mem_v,
           page_table, w_in, w_idx_qb, idx_ln_g, idx_ln_b, hgrn_lb_logits, hgrn_norm_g, w_mem_kv, w_proj_a,
           w_proj_b, w_proj_m, w_o, ln1_g, ln1_b, w_router, router_bias, w_exp_in, w_exp_out, w_sh_in,
           w_sh_out, ln2_g, ln2_b):
    assert x_prompt.shape[0] == 1 and w_in.shape[0] == DEPTH
    _, L, D = x_prompt.shape
    B, T, _ = x_sample.shape
    RS = B * T
    R = L + RS
    n_phys, page = cache_k.shape[1], cache_k.shape[2]
    n_pages = page_table.shape[1]
    past = n_pages * page
    n_mem = mem_prompt.shape[1]

    x = jnp.concatenate([x_prompt[0], x_sample.reshape(RS, D)], axis=0)
    n_a = C_WIDX + H_IDX
    w_pad = jnp.concatenate([w_in[0][:, :n_a].astype(BF16), jnp.zeros((D, LANE - H_IDX), BF16),
                             w_in[0][:, n_a:].astype(BF16)], axis=1)
    t_row = _largest_tile(R, (512, 256, 128, 64, 32, 16))
    z = matmul(x.astype(BF16), w_pad, t_row, _largest_tile(w_pad.shape[1], (640, 512, 256, 128)))
    pos = jnp.concatenate([jnp.arange(L, dtype=I32), past + jnp.arange(RS, dtype=I32) % T])
    tables = _rope_tables(pos)
    t_prep = _largest_tile(R, (256, 128, 64, 32, 16))
    q_rot, k_rot, k_idx = prep_qk(z, tables, idx_ln_g[0], idx_ln_b[0], t_prep)
    v = z[:, C_V:C_V + W_A]
    k_rot_b, v_b, k_idx_b = k_rot.astype(BF16), v.astype(BF16), k_idx.astype(BF16)
    q_idx = qidx_proj(z, w_idx_qb[0].astype(BF16), tables, t_prep)

    kc_p = _largest_tile(L, (512, 256, 128))
    scores_p = idx_scores_prompt(q_idx, z, k_idx_b[:L], L, 128, kc_p)
    bias_p = select_bias(scores_p, jnp.arange(L, dtype=I32), min(TOPK_MAX, L // 4), 128, kc_p)
    o_a_p = attn_prompt(q_rot[:L], k_rot_b[:L], v_b[:L], bias_p, _largest_tile(L, (256, 128)), kc_p)

    q_s = q_idx[:, L:, :].reshape(H_IDX, B, T, D_IDX).transpose(1, 2, 0, 3).reshape(B, T * H_IDX, D_IDX)
    w_s = z[L:, C_WIDX:C_WIDX + H_IDX].reshape(B, T * H_IDX, 1)
    pad_new = lambda a: jnp.zeros((B, page, a.shape[-1]), BF16).at[:, :T].set(a.reshape(B, T, a.shape[-1]))
    scores_s = idx_scores_sample(q_s, w_s, cache_idx_k[0], pad_new(k_idx_b[L:]), page_table)
    width_s = past + page
    lim_s = past + jnp.arange(RS, dtype=I32) % T
    bias_s = select_bias(scores_s.reshape(RS, width_s), lim_s, min(TOPK_MAX, (past + T) // 4),
                         _largest_tile(RS, (128, 64, 32, 16, 8)), LANE)
    bias_x = jnp.broadcast_to(bias_s.reshape(B, 1, T, width_s), (B, H_A, T, width_s)).reshape(B, H_A * T, width_s)
    eye_a = jnp.eye(H_A, dtype=BF16)
    q_bd = jnp.einsum("btgd,hg->bhtgd", q_rot[L:].reshape(B, T, H_A, DH_A), eye_a).reshape(B, H_A * T, W_A)
    o_a_s = attn_sample(q_bd, cache_k[0].reshape(n_phys, page, W_A), cache_v[0].reshape(n_phys, page, W_A),
                        pad_new(k_rot_b[L:]), pad_new(v_b[L:]), bias_x, page_table, T)

    lb = jnp.cumsum(jax.nn.softmax(hgrn_lb_logits.astype(F32), axis=0), axis=0)[0]
    chunk = _largest_tile(L, (128,))
    o_b_p, s_p = hgrn_prompt(z, L, lb, hgrn_norm_g[0], _largest_tile(L, (512, 256, 128)), chunk, 32)
    o_b_s, s_s = hgrn_sample(z[L:, C_FB:C_QM], state_hgrn[0], lb, hgrn_norm_g[0], T, 8 // T)

    mkv = matmul(mem_prompt[0].astype(BF16), w_mem_kv[0].astype(BF16),
                 _largest_tile(n_mem, (256, 128, 64, 32, 16)), _largest_tile(2 * W_M, (512,)))
    mk, mv = mkv[:, :W_M], mkv[:, W_M:]
    o_m_p = mem_attn_prompt(z, L, mk.astype(BF16), mv.astype(BF16), _largest_tile(L, (512, 256, 128)))
    eye_m = jnp.eye(H_M, dtype=BF16)
    qm_bd = jnp.einsum("btgd,hg->bhtgd", z[L:, C_QM:C_QM + W_M].astype(BF16).reshape(B, T, H_M, DH_M),
                       eye_m).reshape(B, H_M * T, W_M)
    o_m_s = mem_attn_sample(qm_bd, cache_mem_k[0].reshape(B, n_mem, W_M), cache_mem_v[0].reshape(B, n_mem, W_M), T)

    o_a = jnp.concatenate([o_a_p, o_a_s.reshape(RS, W_A).astype(BF16)], axis=0)
    o_b = jnp.concatenate([o_b_p, o_b_s.astype(BF16)], axis=0)
    o_m = jnp.concatenate([o_m_p, o_m_s.reshape(RS, W_M).astype(BF16)], axis=0)
    h = gated_proj(o_a, o_b, o_m, w_proj_a[0].astype(BF16), w_proj_b[0].astype(BF16), w_proj_m[0].astype(BF16),
                   z, D, t_row, 256)
    x1, x1b = out_proj_ln(h, w_o[0].astype(BF16), x, ln1_g[0], ln1_b[0], t_prep, _largest_tile(D, (512, 256)))
    y = moe(x1, x1b, w_router[0], router_bias[0], w_exp_in[0], w_exp_out[0], w_sh_in[0], w_sh_out[0],
            ln2_g[0], ln2_b[0])

    return (y[:L].reshape(1, L, D), y[L:].reshape(B, T, D),
            k_rot[:L].reshape(1, 1, L, H_A, DH_A), v[:L].reshape(1, 1, L, H_A, DH_A),
            k_idx[:L].reshape(1, 1, L, D_IDX), s_p.reshape(1, 1, H_B, DK_B, DV_B),
            mk.reshape(1, 1, n_mem, H_M, DH_M), mv.reshape(1, 1, n_mem, H_M, DH_M),
            k_rot[L:].reshape(1, B, T, H_A, DH_A), v[L:].reshape(1, B, T, H_A, DH_A),
            k_idx[L:].reshape(1, B, T, D_IDX), s_s.reshape(1, B, H_B, DK_B, DV_B))
```

```python
import functools

import jax
import jax.numpy as jnp
import numpy as np
from jax import lax
from jax.experimental import pallas as pl
from jax.experimental.pallas import tpu as pltpu

F32 = jnp.float32
BF16 = jnp.bfloat16
I32 = jnp.int32

H_A = 16
DH_A = 128
W_A = H_A * DH_A
H_IDX = 32
D_IDX = 128
R_IDX = 512
IDX_ROPE = 64
IDX_W_SCALE = (H_IDX ** -0.5) * (D_IDX ** -0.5)
TOPK_MAX = 256
H_B = 8
DK_B = 128
DV_B = 128
W_B = H_B * DV_B
H_M = 4
DH_M = 256
W_M = H_M * DH_M
N_EXP = 64
TOP_K = 8
N_GROUP = 8
TOPK_GROUP = 4
ROUTED_SCALE = 2.5
ROPE_THETA = 10000.0
LN_EPS = 1e-5
N_BRANCH = 3
DEPTH = 1
DN_ALPHA = (2.0 * DEPTH) ** 0.25

LANE = 128
SUBLANE = 8
Q_ROWS = 16
NEG = -1e30
VMEM_LIMIT = 56 * 1024 * 1024

C_Q = 0
C_K = W_A
C_V = 2 * W_A
C_CIDX = 3 * W_A
C_KIDX = C_CIDX + R_IDX
C_WIDX = C_KIDX + D_IDX
C_FB = C_WIDX + LANE
C_IB = C_FB + H_B * DK_B
C_QB = C_IB + W_B
C_GB = C_QB + H_B * DK_B
C_QM = C_GB + W_B
C_GATE = C_QM + W_M

NT = (((1,), (1,)), ((), ()))


def _cparams(sem, vmem=VMEM_LIMIT):
    return pltpu.CompilerParams(dimension_semantics=sem, vmem_limit_bytes=vmem)


def _mm_kernel(x_ref, w_ref, o_ref):
    o_ref[...] = jnp.dot(x_ref[...], w_ref[...].astype(BF16),
                         preferred_element_type=F32).astype(o_ref.dtype)


def matmul(x, w, tm, tn, out_dtype=F32):
    m, k = x.shape
    n = w.shape[1]
    return pl.pallas_call(
        _mm_kernel,
        grid=(m // tm, n // tn),
        in_specs=[pl.BlockSpec((tm, k), lambda i, j: (i, 0)),
                  pl.BlockSpec((k, tn), lambda i, j: (0, j))],
        out_specs=pl.BlockSpec((tm, tn), lambda i, j: (i, j)),
        out_shape=jax.ShapeDtypeStruct((m, n), out_dtype),
        compiler_params=_cparams(("parallel", "parallel")),
        name="matmul",
    )(x, w)


def _rope_tables(pos):
    posf = pos.astype(F32)[:, None]
    half = DH_A // 2
    inv = ROPE_THETA ** (-jnp.arange(half, dtype=F32) / half)
    ang = posf * inv[None, :]
    c, s = jnp.cos(ang), jnp.sin(ang)
    cos_a = jnp.concatenate([c, c], axis=-1)
    sin_a = jnp.concatenate([-s, s], axis=-1)
    half_i = IDX_ROPE // 2
    inv_i = ROPE_THETA ** (-jnp.arange(half_i, dtype=F32) / half_i)
    ang_i = posf * inv_i[None, :]
    ci, si = jnp.cos(ang_i), jnp.sin(ang_i)
    one = jnp.ones((pos.shape[0], D_IDX - IDX_ROPE), F32)
    zero_h = jnp.zeros_like(si)
    zero_r = jnp.zeros_like(one)
    cos_i = jnp.concatenate([ci, ci, one], axis=-1)
    sin_dn = jnp.concatenate([zero_h, si, zero_r], axis=-1)
    sin_up = jnp.concatenate([-si, zero_h, zero_r], axis=-1)
    return cos_a, sin_a, cos_i, sin_dn, sin_up


def _rope_full(x, cos, sin):
    return x * cos + pltpu.roll(x, DH_A // 2, 1) * sin


def _rope_idx(x, cos, sin_dn, sin_up):
    h = IDX_ROPE // 2
    return x * cos + pltpu.roll(x, h, 1) * sin_dn + pltpu.roll(x, D_IDX - h, 1) * sin_up


def _prep_kernel(q_ref, k_ref, v_ref, ki_ref, cos_ref, sin_ref, cosi_ref, sdn_ref, sup_ref, g_ref, b_ref,
                 qo_ref, ko_ref, kb_ref, vo_ref, vt_ref, kio_ref, kib_ref):
    cos = cos_ref[...]
    sin = sin_ref[...]
    for h in range(H_A):
        sl = slice(h * DH_A, (h + 1) * DH_A)
        qo_ref[:, sl] = _rope_full(q_ref[:, sl], cos, sin).astype(qo_ref.dtype)
        kr = _rope_full(k_ref[:, sl], cos, sin)
        ko_ref[:, sl] = kr
        kb_ref[:, sl] = kr.astype(kb_ref.dtype)
        v = v_ref[:, sl]
        vo_ref[:, sl] = v
        vt_ref[sl, :] = v.T.astype(vt_ref.dtype)
    x = ki_ref[...]
    mu = jnp.mean(x, axis=-1, keepdims=True)
    var = jnp.mean(jnp.square(x - mu), axis=-1, keepdims=True)
    xn = (x - mu) * lax.rsqrt(var + LN_EPS) * g_ref[...] + b_ref[...]
    ki = _rope_idx(xn, cosi_ref[...], sdn_ref[...], sup_ref[...])
    kio_ref[...] = ki
    kib_ref[...] = ki.astype(kib_ref.dtype)


def prep_qkv(z, tables, ln_g, ln_b, tr):
    rows = z.shape[0]
    cos_a, sin_a, cos_i, sin_dn, sin_up = tables
    row_blk = lambda w, c: pl.BlockSpec((tr, w), lambda i: (i, c))
    tab = pl.BlockSpec((tr, LANE), lambda i: (i, 0))
    vec = pl.BlockSpec((1, LANE), lambda i: (0, 0))
    wide = pl.BlockSpec((tr, W_A), lambda i: (i, 0))
    narrow = pl.BlockSpec((tr, D_IDX), lambda i: (i, 0))
    return pl.pallas_call(
        _prep_kernel,
        grid=(rows // tr,),
        in_specs=[row_blk(W_A, C_Q // W_A), row_blk(W_A, C_K // W_A), row_blk(W_A, C_V // W_A),
                  row_blk(D_IDX, C_KIDX // D_IDX), tab, tab, tab, tab, tab, vec, vec],
        out_specs=[wide, wide, wide, wide, pl.BlockSpec((W_A, tr), lambda i: (0, i)), narrow, narrow],
        out_shape=[jax.ShapeDtypeStruct((rows, W_A), BF16),
                   jax.ShapeDtypeStruct((rows, W_A), F32),
                   jax.ShapeDtypeStruct((rows, W_A), BF16),
                   jax.ShapeDtypeStruct((rows, W_A), F32),
                   jax.ShapeDtypeStruct((W_A, rows), BF16),
                   jax.ShapeDtypeStruct((rows, D_IDX), F32),
                   jax.ShapeDtypeStruct((rows, D_IDX), BF16)],
        compiler_params=_cparams(("parallel",)),
        name="prep_qkv",
    )(z, z, z, z, cos_a, sin_a, cos_i, sin_dn, sin_up, ln_g.reshape(1, D_IDX), ln_b.reshape(1, D_IDX))


def _qidx_kernel(c_ref, w_ref, cosi_ref, sdn_ref, sup_ref, o_ref):
    acc = jnp.dot(c_ref[...].astype(BF16), w_ref[...], preferred_element_type=F32)
    cos, sdn, sup = cosi_ref[...], sdn_ref[...], sup_ref[...]
    for h in range(H_IDX):
        o_ref[h] = _rope_idx(acc[:, h * D_IDX:(h + 1) * D_IDX], cos, sdn, sup).astype(o_ref.dtype)


def qidx_proj(z, w_qb, tables, tr):
    rows = z.shape[0]
    _, _, cos_i, sin_dn, sin_up = tables
    tab = pl.BlockSpec((tr, LANE), lambda i: (i, 0))
    return pl.pallas_call(
        _qidx_kernel,
        grid=(rows // tr,),
        in_specs=[pl.BlockSpec((tr, R_IDX), lambda i: (i, C_CIDX // R_IDX)),
                  pl.BlockSpec((R_IDX, H_IDX * D_IDX), lambda i: (0, 0)),
                  tab, tab, tab],
        out_specs=pl.BlockSpec((H_IDX, tr, D_IDX), lambda i: (0, i, 0)),
        out_shape=jax.ShapeDtypeStruct((H_IDX, rows, D_IDX), BF16),
        compiler_params=_cparams(("parallel",)),
        name="qidx_proj",
    )(z, w_qb, cos_i, sin_dn, sin_up)


def _idx_score_prompt_kernel(q_ref, w_ref, k_ref, o_ref, *, tq, kc, ks):
    i, j = pl.program_id(0), pl.program_id(1)

    @pl.when(j * kc <= i * tq + tq - 1)
    def _():
        for u in range(kc // ks):
            k = k_ref[u * ks:(u + 1) * ks, :]
            acc = jnp.zeros((ks, tq), F32)
            for h in range(H_IDX):
                s = lax.dot_general(k, q_ref[h], NT, preferred_element_type=F32)
                acc = acc + jnp.maximum(s, 0.0) * w_ref[h:h + 1, :]
            o_ref[u * ks:(u + 1) * ks, :] = acc

    @pl.when(j * kc > i * tq + tq - 1)
    def _():
        o_ref[...] = jnp.zeros(o_ref.shape, F32)


def idx_scores_prompt(q_idx, w_t, k_idx_b, L, tq, kc, ks):
    def kmap(i, j):
        return (jnp.minimum(j, (i * tq + tq - 1) // kc), 0)

    return pl.pallas_call(
        functools.partial(_idx_score_prompt_kernel, tq=tq, kc=kc, ks=ks),
        grid=(L // tq, L // kc),
        in_specs=[pl.BlockSpec((H_IDX, tq, D_IDX), lambda i, j: (0, i, 0)),
                  pl.BlockSpec((H_IDX, tq), lambda i, j: (0, i)),
                  pl.BlockSpec((kc, D_IDX), kmap)],
        out_specs=pl.BlockSpec((kc, tq), lambda i, j: (j, i)),
        out_shape=jax.ShapeDtypeStruct((L, L), F32),
        compiler_params=_cparams(("parallel", "arbitrary")),
        name="idx_scores_prompt",
    )(q_idx, w_t, k_idx_b)


def _idx_score_sample_kernel(pt_ref, q_ref, w_ref, *refs, n_pages, n_new):
    page_refs = refs[:n_pages]
    knew_ref = refs[n_pages]
    o_ref = refs[n_pages + 1]
    q = q_ref[...]
    w = w_ref[...] * IDX_W_SCALE
    page = page_refs[0].shape[0]

    def score(k):
        s = lax.dot_general(q, k, NT, preferred_element_type=F32)
        s = jnp.maximum(s, 0.0) * w
        return jnp.sum(s.reshape(n_new, H_IDX, k.shape[0]), axis=1)

    for p in range(n_pages):
        o_ref[:, p * page:(p + 1) * page] = score(page_refs[p][...].astype(BF16))
    o_ref[:, n_pages * page:] = score(knew_ref[...])


def idx_scores_sample(q_s, w_s, cache_idx, knew_idx, page_table):
    nb, rows, _ = q_s.shape
    n_new = rows // H_IDX
    n_pages = page_table.shape[1]
    page = cache_idx.shape[1]
    page_specs = [pl.BlockSpec((None, page, D_IDX), functools.partial(lambda b, pt, p: (pt[b, p], 0, 0), p=p))
                  for p in range(n_pages)]
    grid_spec = pltpu.PrefetchScalarGridSpec(
        num_scalar_prefetch=1,
        grid=(nb,),
        in_specs=[pl.BlockSpec((None, rows, D_IDX), lambda b, pt: (b, 0, 0)),
                  pl.BlockSpec((None, rows, 1), lambda b, pt: (b, 0, 0))] + page_specs +
                 [pl.BlockSpec((None, page, D_IDX), lambda b, pt: (b, 0, 0))],
        out_specs=pl.BlockSpec((None, n_new, (n_pages + 1) * page), lambda b, pt: (b, 0, 0)),
    )
    return pl.pallas_call(
        functools.partial(_idx_score_sample_kernel, n_pages=n_pages, n_new=n_new),
        grid_spec=grid_spec,
        out_shape=jax.ShapeDtypeStruct((nb, n_new, (n_pages + 1) * page), F32),
        compiler_params=_cparams(("arbitrary",)),
        name="idx_scores_sample",
    )(page_table, q_s, w_s, *([cache_idx] * n_pages), knew_idx)


def _key_to_float(key):
    bits = key ^ (jnp.right_shift(key, 31) & jnp.int32(0x7FFFFFFF))
    return lax.bitcast_convert_type(bits, F32)


def _select_kernel(nch_ref, lim_ref, s_ref, o_ref, m_ref, *, n_sel, kc):
    width, tq = s_ref.shape
    n_chunks = nch_ref[pl.program_id(0)]
    lim = lim_ref[...]
    kq = jnp.minimum(lim + 1, n_sel).astype(F32)

    def mask_chunk(c, carry):
        rows = pl.ds(pl.multiple_of(c * kc, kc), kc)
        row = c * kc + lax.broadcasted_iota(I32, (kc, tq), 0)
        m_ref[rows, :] = jnp.where(row <= lim, s_ref[rows, :], -jnp.inf)
        return carry

    lax.fori_loop(0, n_chunks, mask_chunk, 0)

    acc_rows = min(kc, 8 * SUBLANE)

    def count_ge(thr):
        def body(c, acc):
            blk = m_ref[pl.ds(pl.multiple_of(c * kc, kc), kc), :]
            hit = jnp.where(blk >= thr, 1.0, 0.0)
            return acc + jnp.sum(hit.reshape(kc // acc_rows, acc_rows, tq), axis=0)

        acc = lax.fori_loop(0, n_chunks, body, jnp.zeros((acc_rows, tq), F32))
        return jnp.sum(acc, axis=0, keepdims=True)

    def bit_step(it, key):
        cand = key + jnp.left_shift(jnp.int32(1), 31 - it)
        ok = count_ge(_key_to_float(cand)) >= kq
        return jnp.where(ok, cand, key)

    key = lax.fori_loop(0, 32, bit_step, jnp.full((1, tq), jnp.iinfo(jnp.int32).min, I32))
    thr = _key_to_float(key)

    def out_chunk(c, carry):
        rows = pl.ds(pl.multiple_of(c * kc, kc), kc)
        o_ref[rows, :] = jnp.where(m_ref[rows, :] >= thr, 0.0, NEG)
        return carry

    lax.fori_loop(0, n_chunks, out_chunk, 0)

    def fill_chunk(c, carry):
        o_ref[pl.ds(pl.multiple_of(c * kc, kc), kc), :] = jnp.full((kc, tq), NEG, F32)
        return carry

    lax.fori_loop(n_chunks, width // kc, fill_chunk, 0)


def select_bias(scores_t, lim, n_sel, tq, kc):
    width, n_q = scores_t.shape
    n_tiles = n_q // tq
    nch = (jnp.max(lim.reshape(n_tiles, tq), axis=1) + kc) // kc
    grid_spec = pltpu.PrefetchScalarGridSpec(
        num_scalar_prefetch=1,
        grid=(n_tiles,),
        in_specs=[pl.BlockSpec((1, tq), lambda i, n: (0, i)),
                  pl.BlockSpec((width, tq), lambda i, n: (0, i))],
        out_specs=pl.BlockSpec((width, tq), lambda i, n: (0, i)),
        scratch_shapes=[pltpu.VMEM((width, tq), F32)],
    )
    return pl.pallas_call(
        functools.partial(_select_kernel, n_sel=n_sel, kc=kc),
        grid_spec=grid_spec,
        out_shape=jax.ShapeDtypeStruct((width, n_q), F32),
        compiler_params=_cparams(("arbitrary",)),
        name="select_bias",
    )(nch.astype(I32), lim.reshape(1, n_q).astype(I32), scores_t)


def _attn_prompt_kernel(q_ref, k_ref, vt_ref, b_ref, o_ref, m_sc, l_sc, acc_sc, *, tq, tk):
    i, j = pl.program_id(0), pl.program_id(1)
    last = (i * tq + tq - 1) // tk
    scale = DH_A ** -0.5

    @pl.when(j == 0)
    def _():
        m_sc[...] = jnp.full(m_sc.shape, NEG, F32)
        l_sc[...] = jnp.zeros(l_sc.shape, F32)
        acc_sc[...] = jnp.zeros(acc_sc.shape, F32)

    @pl.when(j <= last)
    def _():
        for h in range(H_A):
            sl = slice(h * DH_A, (h + 1) * DH_A)
            q_h = q_ref[:, sl]
            m, l, acc = m_sc[h], l_sc[h], acc_sc[h]
            for u in range(tk // LANE):
                keys = slice(u * LANE, (u + 1) * LANE)
                s = lax.dot_general(k_ref[keys, sl], q_h, NT, preferred_element_type=F32) * scale + b_ref[keys, :]
                m_new = jnp.maximum(m, jnp.max(s, axis=0, keepdims=True))
                alpha = jnp.exp(m - m_new)
                p = jnp.exp(s - m_new)
                l = alpha * l + jnp.sum(p, axis=0, keepdims=True)
                acc = alpha * acc + jnp.dot(vt_ref[sl, keys], p.astype(BF16), preferred_element_type=F32)
                m = m_new
            m_sc[h], l_sc[h], acc_sc[h] = m, l, acc

    @pl.when(j == last)
    def _():
        for h in range(H_A):
            o_ref[:, h * DH_A:(h + 1) * DH_A] = (acc_sc[h] / l_sc[h]).T.astype(o_ref.dtype)


def attn_prompt(q, k, v_t, bias_t, L, tq, tk):
    def jmap(i, j):
        return jnp.minimum(j, (i * tq + tq - 1) // tk)

    return pl.pallas_call(
        functools.partial(_attn_prompt_kernel, tq=tq, tk=tk),
        grid=(L // tq, L // tk),
        in_specs=[pl.BlockSpec((tq, W_A), lambda i, j: (i, 0)),
                  pl.BlockSpec((tk, W_A), lambda i, j: (jmap(i, j), 0)),
                  pl.BlockSpec((W_A, tk), lambda i, j: (0, jmap(i, j))),
                  pl.BlockSpec((tk, tq), lambda i, j: (jmap(i, j), i))],
        out_specs=pl.BlockSpec((tq, W_A), lambda i, j: (i, 0)),
        out_shape=jax.ShapeDtypeStruct((L, W_A), BF16),
        scratch_shapes=[pltpu.VMEM((H_A, 1, tq), F32), pltpu.VMEM((H_A, 1, tq), F32),
                        pltpu.VMEM((H_A, DH_A, tq), F32)],
        compiler_params=_cparams(("parallel", "arbitrary")),
        name="attn_prompt",
    )(q, k, v_t, bias_t)


def _attn_sample_kernel(pt_ref, q_ref, kc_ref, vc_ref, kn_ref, vn_ref, b_ref, o_ref, m_sc, l_sc, acc_sc,
                        *, n_pages, n_new):
    p = pl.program_id(1)
    scale = DH_A ** -0.5

    @pl.when(p == 0)
    def _():
        m_sc[...] = jnp.full(m_sc.shape, NEG, F32)
        l_sc[...] = jnp.zeros(l_sc.shape, F32)
        acc_sc[...] = jnp.zeros(acc_sc.shape, F32)

    def update(keys, values):
        s = jnp.stack([lax.dot_general(q_ref[h], keys(h), NT, preferred_element_type=F32) for h in range(H_A)])
        s = s * scale + b_ref[...][None]
        m_prev = m_sc[...]
        m_new = jnp.maximum(m_prev, jnp.max(s, axis=-1, keepdims=True))
        alpha = jnp.exp(m_prev - m_new)
        pr = jnp.exp(s - m_new)
        l_sc[...] = alpha * l_sc[...] + jnp.sum(pr, axis=-1, keepdims=True)
        pb = pr.astype(BF16)
        pv = jnp.stack([jnp.dot(pb[h], values(h), preferred_element_type=F32) for h in range(H_A)])
        acc_sc[...] = alpha * acc_sc[...] + pv
        m_sc[...] = m_new

    @pl.when(p < n_pages)
    def _():
        page = kc_ref.shape[0] // H_A
        update(lambda h: kc_ref[pl.ds(h, page, stride=H_A), :].astype(BF16),
               lambda h: vc_ref[pl.ds(h, page, stride=H_A), :].astype(BF16))

    @pl.when(p == n_pages)
    def _():
        update(lambda h: kn_ref[:, h * DH_A:(h + 1) * DH_A], lambda h: vn_ref[:, h * DH_A:(h + 1) * DH_A])
        out = acc_sc[...] / l_sc[...]
        for h in range(H_A):
            o_ref[:, h * DH_A:(h + 1) * DH_A] = out[h, :n_new]


def attn_sample(q_h, cache_k, cache_v, knew, vnew, bias, page_table, n_new):
    nb = q_h.shape[0]
    n_pages = page_table.shape[1]
    page = cache_k.shape[1] // H_A

    def cmap(b, p, pt):
        return (pt[b, jnp.minimum(p, n_pages - 1)], 0, 0)

    new_spec = pl.BlockSpec((None, page, W_A), lambda b, p, pt: (b, 0, 0))
    cache_spec = pl.BlockSpec((None, page * H_A, DH_A), cmap)
    grid_spec = pltpu.PrefetchScalarGridSpec(
        num_scalar_prefetch=1,
        grid=(nb, n_pages + 1),
        in_specs=[pl.BlockSpec((None, H_A, Q_ROWS, DH_A), lambda b, p, pt: (b, 0, 0, 0)),
                  cache_spec, cache_spec, new_spec, new_spec,
                  pl.BlockSpec((None, Q_ROWS, page), lambda b, p, pt: (b, 0, p))],
        out_specs=pl.BlockSpec((None, n_new, W_A), lambda b, p, pt: (b, 0, 0)),
        scratch_shapes=[pltpu.VMEM((H_A, Q_ROWS, 1), F32), pltpu.VMEM((H_A, Q_ROWS, 1), F32),
                        pltpu.VMEM((H_A, Q_ROWS, DH_A), F32)],
    )
    return pl.pallas_call(
        functools.partial(_attn_sample_kernel, n_pages=n_pages, n_new=n_new),
        grid_spec=grid_spec,
        out_shape=jax.ShapeDtypeStruct((nb, n_new, W_A), F32),
        compiler_params=_cparams(("parallel", "arbitrary")),
        name="attn_sample",
    )(page_table, q_h, cache_k, cache_v, knew, vnew, bias)


def _hgrn_chunk(fr, ic, qc, lb, st, sub, row_valid=None):
    C = fr.shape[0]
    lf = jnp.log(lb + (1.0 - lb) * jax.nn.sigmoid(fr))
    kk = (1.0 - lb) * jax.nn.sigmoid(-fr)
    if row_valid is not None:
        lf = jnp.where(row_valid, lf, 0.0)
        kk = jnp.where(row_valid, kk, 0.0)
    r = lax.broadcasted_iota(I32, (C, C), 0)
    c = lax.broadcasted_iota(I32, (C, C), 1)
    tri = jnp.where(r >= c, 1.0, 0.0).astype(F32)
    G = jnp.dot(tri, lf, preferred_element_type=F32, precision=lax.Precision.HIGHEST)
    o_inter = lax.dot_general((qc * jnp.exp(G)).astype(BF16), st.astype(BF16), NT, preferred_element_type=F32)
    ic_b = ic.astype(BF16)
    t3 = lax.broadcasted_iota(I32, (sub, sub, DK_B), 0)
    s3 = lax.broadcasted_iota(I32, (sub, sub, DK_B), 1)
    pieces = []
    for a in range(C // sub):
        r0 = a * sub
        Gs, ks, qs, is_ = G[r0:r0 + sub], kk[r0:r0 + sub], qc[r0:r0 + sub], ic[r0:r0 + sub]
        dec = jnp.exp(jnp.where(t3 >= s3, Gs[:, None, :] - Gs[None, :, :], -jnp.inf))
        a3 = jnp.sum(dec * ks[None, :, :] * qs[:, None, :], axis=-1, keepdims=True)
        o_a = jnp.sum(a3 * is_[None, :, :], axis=1)
        if a > 0:
            gb = G[r0 - 1:r0]
            qa = (qs * jnp.exp(Gs - gb)).astype(BF16)
            kb = (kk[:r0] * jnp.exp(gb - G[:r0])).astype(BF16)
            A = lax.dot_general(qa, kb, NT, preferred_element_type=F32)
            o_a = o_a + jnp.dot(A.astype(BF16), ic_b[:r0], preferred_element_type=F32)
        pieces.append(o_a)
    o = o_inter + (pieces[0] if len(pieces) == 1 else jnp.concatenate(pieces, axis=0))
    g_last = G[C - 1:C]
    kd = (kk * jnp.exp(g_last - G)).astype(BF16)
    st_new = st * jnp.exp(g_last) + lax.dot_general(ic_b, kd, (((0,), (0,)), ((), ())),
                                                    preferred_element_type=F32)
    return o, st_new


def _hgrn_finish(o, gate, norm_g):
    o = o * lax.rsqrt(jnp.mean(jnp.square(o), axis=-1, keepdims=True) + LN_EPS)
    return o * norm_g * (gate * jax.nn.sigmoid(gate))


def _hgrn_prompt_kernel(f_ref, i_ref, q_ref, g_ref, lb_ref, ng_ref, o_ref, s_ref, st_sc, *, chunk, sub):
    j = pl.program_id(1)

    @pl.when(j == 0)
    def _():
        st_sc[...] = jnp.zeros(st_sc.shape, F32)

    lb = lb_ref[...]
    ng = ng_ref[...]

    def body(c, carry):
        rows = pl.ds(pl.multiple_of(c * chunk, chunk), chunk)
        o, st_new = _hgrn_chunk(f_ref[rows, :], i_ref[rows, :], q_ref[rows, :], lb, st_sc[...], sub)
        st_sc[...] = st_new
        o_ref[rows, :] = _hgrn_finish(o, g_ref[rows, :], ng).astype(o_ref.dtype)
        return carry

    lax.fori_loop(0, f_ref.shape[0] // chunk, body, 0)

    @pl.when(j == pl.num_programs(1) - 1)
    def _():
        s_ref[...] = st_sc[...].T


def hgrn_prompt(z, n_rows, lb, norm_g, tb, chunk, sub):
    blk = lambda c0: pl.BlockSpec((tb, DK_B), lambda h, j, c0=c0: (j, c0 // DK_B + h))
    vec = pl.BlockSpec((1, DK_B), lambda h, j: (0, h))
    return pl.pallas_call(
        functools.partial(_hgrn_prompt_kernel, chunk=chunk, sub=sub),
        grid=(H_B, n_rows // tb),
        in_specs=[blk(C_FB), blk(C_IB), blk(C_QB), blk(C_GB), vec,
                  pl.BlockSpec((1, DV_B), lambda h, j: (0, 0))],
        out_specs=[pl.BlockSpec((tb, DV_B), lambda h, j: (j, h)),
                   pl.BlockSpec((None, DK_B, DV_B), lambda h, j: (h, 0, 0))],
        out_shape=[jax.ShapeDtypeStruct((n_rows, W_B), BF16),
                   jax.ShapeDtypeStruct((H_B, DK_B, DV_B), F32)],
        scratch_shapes=[pltpu.VMEM((DV_B, DK_B), F32)],
        compiler_params=_cparams(("parallel", "arbitrary")),
        name="hgrn_prompt",
    )(z, z, z, z, lb.reshape(1, H_B * DK_B), norm_g.reshape(1, DV_B))


def _hgrn_sample_kernel(x_ref, s_ref, lb_ref, ng_ref, o_ref, so_ref, *, n_new, seqs):
    rows = seqs * n_new
    rid = lax.broadcasted_iota(I32, (rows, DK_B), 0)
    ng = ng_ref[...]
    for h in range(H_B):
        col = lambda g: slice(g * W_B + h * DK_B, g * W_B + (h + 1) * DK_B)
        fr, ic, qc, gate = x_ref[:, col(0)], x_ref[:, col(1)], x_ref[:, col(2)], x_ref[:, col(3)]
        lb = lb_ref[:, h * DK_B:(h + 1) * DK_B]
        o = jnp.zeros((rows, DV_B), F32)
        for b in range(seqs):
            valid = (rid >= b * n_new) & (rid < (b + 1) * n_new)
            ob, st_new = _hgrn_chunk(fr, ic, qc, lb, s_ref[b, h].T, rows, row_valid=valid)
            so_ref[b, h] = st_new.T
            o = jnp.where(valid, ob, o)
        o_ref[:, h * DV_B:(h + 1) * DV_B] = _hgrn_finish(o, gate, ng).astype(o_ref.dtype)


def hgrn_sample(xs, state, lb, norm_g, n_new, seqs):
    nb = state.shape[0]
    rows = seqs * n_new
    return pl.pallas_call(
        functools.partial(_hgrn_sample_kernel, n_new=n_new, seqs=seqs),
        grid=(nb // seqs,),
        in_specs=[pl.BlockSpec((rows, 4 * W_B), lambda i: (i, 0)),
                  pl.BlockSpec((seqs, H_B, DK_B, DV_B), lambda i: (i, 0, 0, 0)),
                  pl.BlockSpec((1, H_B * DK_B), lambda i: (0, 0)),
                  pl.BlockSpec((1, DV_B), lambda i: (0, 0))],
        out_specs=[pl.BlockSpec((rows, W_B), lambda i: (i, 0)),
                   pl.BlockSpec((seqs, H_B, DK_B, DV_B), lambda i: (i, 0, 0, 0))],
        out_shape=[jax.ShapeDtypeStruct((nb * n_new, W_B), F32),
                   jax.ShapeDtypeStruct(state.shape, F32)],
        compiler_params=_cparams(("parallel",)),
        name="hgrn_sample",
    )(xs, state, lb.reshape(1, H_B * DK_B), norm_g.reshape(1, DV_B))


def _softmax_rows(s):
    m = jnp.max(s, axis=-1, keepdims=True)
    p = jnp.exp(s - m)
    return p / jnp.sum(p, axis=-1, keepdims=True)


def _mem_attn_prompt_kernel(*refs):
    q_refs, mk_ref, mv_ref, o_ref = refs[:H_M], refs[H_M], refs[H_M + 1], refs[H_M + 2]
    for h in range(H_M):
        sl = slice(h * DH_M, (h + 1) * DH_M)
        s = lax.dot_general(q_refs[h][...].astype(BF16), mk_ref[:, sl], NT,
                            preferred_element_type=F32) * (DH_M ** -0.5)
        p = _softmax_rows(s).astype(BF16)
        o_ref[:, sl] = jnp.dot(p, mv_ref[:, sl], preferred_element_type=F32).astype(o_ref.dtype)


def mem_attn_prompt(z, n_rows, mk, mv, tr):
    n_mem = mk.shape[0]
    q_specs = [pl.BlockSpec((tr, DH_M), functools.partial(lambda i, h: (i, C_QM // DH_M + h), h=h))
               for h in range(H_M)]
    kv = pl.BlockSpec((n_mem, W_M), lambda i: (0, 0))
    return pl.pallas_call(
        _mem_attn_prompt_kernel,
        grid=(n_rows // tr,),
        in_specs=q_specs + [kv, kv],
        out_specs=pl.BlockSpec((tr, W_M), lambda i: (i, 0)),
        out_shape=jax.ShapeDtypeStruct((n_rows, W_M), BF16),
        compiler_params=_cparams(("parallel",)),
        name="mem_attn_prompt",
    )(*([z] * H_M), mk, mv)


def _mem_attn_sample_kernel(q_ref, k_ref, v_ref, o_ref, *, n_new):
    s = lax.dot_general(q_ref[...], k_ref[...].astype(BF16), NT,
                        preferred_element_type=F32) * (DH_M ** -0.5)
    p = _softmax_rows(s).astype(BF16)
    full = jnp.dot(p, v_ref[...].astype(BF16), preferred_element_type=F32)
    for h in range(H_M):
        o_ref[:, h * DH_M:(h + 1) * DH_M] = full[h * n_new:(h + 1) * n_new, h * DH_M:(h + 1) * DH_M]


def mem_attn_sample(q_bd, cache_mk, cache_mv, n_new):
    nb, rows, _ = q_bd.shape
    n_mem = cache_mk.shape[1]
    return pl.pallas_call(
        functools.partial(_mem_attn_sample_kernel, n_new=n_new),
        grid=(nb,),
        in_specs=[pl.BlockSpec((None, rows, W_M), lambda b: (b, 0, 0)),
                  pl.BlockSpec((None, n_mem, W_M), lambda b: (b, 0, 0)),
                  pl.BlockSpec((None, n_mem, W_M), lambda b: (b, 0, 0))],
        out_specs=pl.BlockSpec((None, n_new, W_M), lambda b: (b, 0, 0)),
        out_shape=jax.ShapeDtypeStruct((nb, n_new, W_M), F32),
        compiler_params=_cparams(("parallel",)),
        name="mem_attn_sample",
    )(q_bd, cache_mk, cache_mv)


def _gated_proj_kernel(oa_ref, ob_ref, om_ref, wa_ref, wb_ref, wm_ref, g0_ref, g1_ref, g2_ref, o_ref):
    def branch(x_ref, w_ref, g_ref):
        return jax.nn.sigmoid(g_ref[...]) * jnp.dot(x_ref[...], w_ref[...], preferred_element_type=F32)

    h = branch(oa_ref, wa_ref, g0_ref) + branch(ob_ref, wb_ref, g1_ref) + branch(om_ref, wm_ref, g2_ref)
    o_ref[...] = h.astype(o_ref.dtype)


def gated_proj(o_a, o_b, o_m, w_a, w_b, w_m, z, d_model, tm, tn):
    rows = o_a.shape[0]
    x_spec = lambda w: pl.BlockSpec((tm, w), lambda i, j: (i, 0))
    w_spec = lambda w: pl.BlockSpec((w, tn), lambda i, j: (0, j))
    g_spec = lambda g: pl.BlockSpec((tm, tn), lambda i, j, g=g: (i, (C_GATE + g * d_model) // tn + j))
    return pl.pallas_call(
        _gated_proj_kernel,
        grid=(rows // tm, d_model // tn),
        in_specs=[x_spec(W_A), x_spec(W_B), x_spec(W_M), w_spec(W_A), w_spec(W_B), w_spec(W_M),
                  g_spec(0), g_spec(1), g_spec(2)],
        out_specs=pl.BlockSpec((tm, tn), lambda i, j: (i, j)),
        out_shape=jax.ShapeDtypeStruct((rows, d_model), BF16),
        compiler_params=_cparams(("parallel", "parallel")),
        name="gated_proj",
    )(o_a, o_b, o_m, w_a, w_b, w_m, z, z, z)


def _layer_norm(x, g, b):
    mu = jnp.mean(x, axis=-1, keepdims=True)
    var = jnp.mean(jnp.square(x - mu), axis=-1, keepdims=True)
    return (x - mu) * lax.rsqrt(var + LN_EPS) * g + b


def _out_proj_ln_kernel(h_ref, w_ref, x_ref, g_ref, b_ref, o_ref, ob_ref, acc_sc):
    k = pl.program_id(1)

    @pl.when(k == 0)
    def _():
        acc_sc[...] = jnp.zeros(acc_sc.shape, F32)

    acc_sc[...] += jnp.dot(h_ref[...], w_ref[...], preferred_element_type=F32)

    @pl.when(k == pl.num_programs(1) - 1)
    def _():
        y = _layer_norm(DN_ALPHA * x_ref[...] + acc_sc[...], g_ref[...], b_ref[...])
        o_ref[...] = y
        ob_ref[...] = y.astype(ob_ref.dtype)


def out_proj_ln(h, w_o, x, ln_g, ln_b, tm, tk):
    rows, d = x.shape
    vec = pl.BlockSpec((1, d), lambda i, k: (0, 0))
    return pl.pallas_call(
        _out_proj_ln_kernel,
        grid=(rows // tm, d // tk),
        in_specs=[pl.BlockSpec((tm, tk), lambda i, k: (i, k)),
                  pl.BlockSpec((tk, d), lambda i, k: (k, 0)),
                  pl.BlockSpec((tm, d), lambda i, k: (i, 0)), vec, vec],
        out_specs=[pl.BlockSpec((tm, d), lambda i, k: (i, 0)),
                   pl.BlockSpec((tm, d), lambda i, k: (i, 0))],
        out_shape=[jax.ShapeDtypeStruct((rows, d), F32), jax.ShapeDtypeStruct((rows, d), BF16)],
        scratch_shapes=[pltpu.VMEM((tm, d), F32)],
        compiler_params=_cparams(("parallel", "arbitrary")),
        name="out_proj_ln",
    )(h, w_o, x, ln_g.reshape(1, d), ln_b.reshape(1, d))


def _router_kernel(x_ref, w_ref, b_ref, sel_ref, exp_ref, idx_ref, wgt_ref, rank_ref, cnt_ref, carry_sc):
    i = pl.program_id(0)

    @pl.when(i == 0)
    def _():
        carry_sc[...] = jnp.zeros(carry_sc.shape, F32)

    tr = x_ref.shape[0]
    per_group = N_EXP // N_GROUP
    lane = lax.broadcasted_iota(I32, (tr, LANE), 1)
    s = jax.nn.sigmoid(jnp.dot(x_ref[...], w_ref[...], preferred_element_type=F32))
    real = lane < N_EXP
    choice = jnp.where(real, s + b_ref[...], 0.0)
    hi = lax.Precision.HIGHEST

    def first_argmax(v):
        mx = jnp.max(v, axis=-1, keepdims=True)
        return mx, jnp.min(jnp.where(v == mx, lane, LANE), axis=-1, keepdims=True)

    members = [jnp.dot(choice, sel_ref[r], preferred_element_type=F32, precision=hi) for r in range(per_group)]
    m1 = functools.reduce(jnp.maximum, members)
    first = functools.reduce(jnp.minimum, [jnp.where(members[r] == m1, r, per_group) for r in range(per_group)])
    m2 = functools.reduce(jnp.maximum, [jnp.where(first == r, -jnp.inf, members[r]) for r in range(per_group)])
    cur = jnp.where(lane < N_GROUP, m1 + m2, -jnp.inf)
    gsel = jnp.zeros((tr, LANE), F32)
    for _ in range(TOPK_GROUP):
        _, gi = first_argmax(cur)
        hit = lane == gi
        gsel = jnp.where(hit, 1.0, gsel)
        cur = jnp.where(hit, -jnp.inf, cur)
    emask = jnp.dot(gsel, exp_ref[...], preferred_element_type=F32, precision=hi)
    masked = jnp.where((emask > 0.5) & real, choice, -jnp.inf)

    idx_out = jnp.zeros((tr, LANE), I32)
    wgt_out = jnp.zeros((tr, LANE), F32)
    onehot = jnp.zeros((tr, LANE), F32)
    picks = []
    for k in range(TOP_K):
        _, ei = first_argmax(masked)
        hit = lane == ei
        wk = jnp.sum(jnp.where(hit, s, 0.0), axis=-1, keepdims=True)
        masked = jnp.where(hit, -jnp.inf, masked)
        onehot = jnp.where(hit, 1.0, onehot)
        idx_out = jnp.where(lane == k, ei, idx_out)
        wgt_out = jnp.where(lane == k, wk, wgt_out)
        picks.append(hit)
    wsum = jnp.sum(wgt_out, axis=-1, keepdims=True)
    wgt_ref[...] = wgt_out / wsum * ROUTED_SCALE
    idx_ref[...] = idx_out

    r = lax.broadcasted_iota(I32, (tr, tr), 0)
    c = lax.broadcasted_iota(I32, (tr, tr), 1)
    before = jnp.where(r > c, 1.0, 0.0).astype(BF16)
    rank = carry_sc[...] + jnp.dot(before, onehot.astype(BF16), preferred_element_type=F32)
    rank_out = jnp.zeros((tr, LANE), F32)
    for k in range(TOP_K):
        rk = jnp.sum(jnp.where(picks[k], rank, 0.0), axis=-1, keepdims=True)
        rank_out = jnp.where(lane == k, rk, rank_out)
    rank_ref[...] = rank_out.astype(I32)
    carry_sc[...] += jnp.sum(onehot, axis=0, keepdims=True)
    cnt_ref[...] = carry_sc[...].astype(I32)


def router(x1b, w_router, router_bias, tr):
    rows, d = x1b.shape
    per_group = N_EXP // N_GROUP
    wr = jnp.pad(w_router, ((0, 0), (0, LANE - N_EXP))).astype(BF16)
    bias = jnp.pad(router_bias.astype(F32), (0, LANE - N_EXP)).reshape(1, LANE)
    e = np.arange(LANE)
    sel = np.stack([(e[:, None] == per_group * e[None, :] + r) & (e[None, :] < N_GROUP)
                    for r in range(per_group)]).astype(np.float32)
    expand = ((e[None, :] // per_group == e[:, None]) & (e[None, :] < N_EXP)).astype(np.float32)
    tile = pl.BlockSpec((tr, LANE), lambda i: (i, 0))
    return pl.pallas_call(
        _router_kernel,
        grid=(rows // tr,),
        in_specs=[pl.BlockSpec((tr, d), lambda i: (i, 0)),
                  pl.BlockSpec((d, LANE), lambda i: (0, 0)),
                  pl.BlockSpec((1, LANE), lambda i: (0, 0)),
                  pl.BlockSpec((per_group, LANE, LANE), lambda i: (0, 0, 0)),
                  pl.BlockSpec((LANE, LANE), lambda i: (0, 0))],
        out_specs=[tile, tile, tile, pl.BlockSpec((1, LANE), lambda i: (0, 0))],
        out_shape=[jax.ShapeDtypeStruct((rows, LANE), I32), jax.ShapeDtypeStruct((rows, LANE), F32),
                   jax.ShapeDtypeStruct((rows, LANE), I32), jax.ShapeDtypeStruct((1, LANE), I32)],
        scratch_shapes=[pltpu.VMEM((1, LANE), F32)],
        compiler_params=_cparams(("arbitrary",)),
        name="router",
    )(x1b, wr, bias, jnp.asarray(sel), jnp.asarray(expand))


GATHER_UNROLL = 8


def _gather_rows_kernel(idx_ref, src_ref, o_ref, buf, sem, *, ts):
    base = pl.program_id(0) * ts

    def issue(r, carry):
        pltpu.make_async_copy(src_ref.at[pl.ds(idx_ref[base + r], 1)], buf.at[pl.ds(r, 1)], sem).start()
        return carry

    lax.fori_loop(0, ts, issue, 0, unroll=GATHER_UNROLL)
    pltpu.make_async_copy(src_ref.at[pl.ds(0, ts)], buf, sem).wait()
    o_ref[...] = buf[...].astype(o_ref.dtype)


def gather_rows(src, idx, ts, out_dtype):
    d = src.shape[1]
    n_out = idx.shape[0]
    grid_spec = pltpu.PrefetchScalarGridSpec(
        num_scalar_prefetch=1,
        grid=(n_out // ts,),
        in_specs=[pl.BlockSpec(memory_space=pl.ANY)],
        out_specs=pl.BlockSpec((ts, d), lambda t, idx: (t, 0)),
        scratch_shapes=[pltpu.VMEM((ts, d), src.dtype), pltpu.SemaphoreType.DMA(())],
    )
    return pl.pallas_call(
        functools.partial(_gather_rows_kernel, ts=ts),
        grid_spec=grid_spec,
        out_shape=jax.ShapeDtypeStruct((n_out, d), out_dtype),
        compiler_params=_cparams(("arbitrary",)),
        name="gather_rows",
    )(idx, src)


def _expert_up_kernel(te_ref, nt_ref, x_ref, wa_ref, wb_ref, o_ref, wa_sc, wb_sc):
    t = pl.program_id(1)
    changed = jnp.logical_or(t == 0, te_ref[t] != te_ref[jnp.maximum(t - 1, 0)])

    @pl.when(jnp.logical_and(changed, t < nt_ref[0]))
    def _():
        wa_sc[...] = wa_ref[...].astype(BF16)
        wb_sc[...] = wb_ref[...].astype(BF16)

    @pl.when(t < nt_ref[0])
    def _():
        x = x_ref[...]
        a = jnp.dot(x, wa_sc[...], preferred_element_type=F32)
        b = jnp.dot(x, wb_sc[...], preferred_element_type=F32)
        o_ref[...] = (a * jax.nn.sigmoid(a) * b).astype(o_ref.dtype)

    @pl.when(t >= nt_ref[0])
    def _():
        o_ref[...] = jnp.zeros(o_ref.shape, o_ref.dtype)


def expert_up(xs, w_in, tile_expert, n_tiles_used, tm, fc):
    n_slots, d = xs.shape
    f = w_in.shape[2] // 2
    nf = f // fc

    def tmap(t, nt):
        return jnp.minimum(t, nt[0] - 1)

    grid_spec = pltpu.PrefetchScalarGridSpec(
        num_scalar_prefetch=2,
        grid=(nf, n_slots // tm),
        in_specs=[pl.BlockSpec((tm, d), lambda fi, t, te, nt: (tmap(t, nt), 0)),
                  pl.BlockSpec((None, d, fc), lambda fi, t, te, nt: (te[tmap(t, nt)], 0, fi)),
                  pl.BlockSpec((None, d, fc), lambda fi, t, te, nt: (te[tmap(t, nt)], 0, nf + fi))],
        out_specs=pl.BlockSpec((tm, fc), lambda fi, t, te, nt: (t, fi)),
        scratch_shapes=[pltpu.VMEM((d, fc), BF16), pltpu.VMEM((d, fc), BF16)],
    )
    return pl.pallas_call(
        _expert_up_kernel,
        grid_spec=grid_spec,
        out_shape=jax.ShapeDtypeStruct((n_slots, f), BF16),
        compiler_params=_cparams(("arbitrary", "arbitrary")),
        name="expert_up",
    )(tile_expert, n_tiles_used, xs, w_in, w_in)


def _expert_down_kernel(te_ref, nt_ref, h_ref, w_ref, o_ref, w_sc):
    t = pl.program_id(1)
    changed = jnp.logical_or(t == 0, te_ref[t] != te_ref[jnp.maximum(t - 1, 0)])

    @pl.when(jnp.logical_and(changed, t < nt_ref[0]))
    def _():
        w_sc[...] = w_ref[...].astype(BF16)

    @pl.when(t < nt_ref[0])
    def _():
        o_ref[...] = jnp.dot(h_ref[...], w_sc[...], preferred_element_type=F32)

    @pl.when(t >= nt_ref[0])
    def _():
        o_ref[...] = jnp.zeros(o_ref.shape, o_ref.dtype)


def expert_down(hs, w_out, tile_expert, n_tiles_used, tm, nc):
    n_slots, f = hs.shape
    d = w_out.shape[2]

    def tmap(t, nt):
        return jnp.minimum(t, nt[0] - 1)

    grid_spec = pltpu.PrefetchScalarGridSpec(
        num_scalar_prefetch=2,
        grid=(d // nc, n_slots // tm),
        in_specs=[pl.BlockSpec((tm, f), lambda j, t, te, nt: (tmap(t, nt), 0)),
                  pl.BlockSpec((None, f, nc), lambda j, t, te, nt: (te[tmap(t, nt)], 0, j))],
        out_specs=pl.BlockSpec((tm, nc), lambda j, t, te, nt: (t, j)),
        scratch_shapes=[pltpu.VMEM((f, nc), BF16)],
    )
    return pl.pallas_call(
        _expert_down_kernel,
        grid_spec=grid_spec,
        out_shape=jax.ShapeDtypeStruct((n_slots, d), F32),
        compiler_params=_cparams(("arbitrary", "arbitrary")),
        name="expert_down",
    )(tile_expert, n_tiles_used, hs, w_out)


def _swiglu_up_kernel(x_ref, wa_ref, wb_ref, o_ref):
    x = x_ref[...]
    a = jnp.dot(x, wa_ref[...].astype(BF16), preferred_element_type=F32)
    b = jnp.dot(x, wb_ref[...].astype(BF16), preferred_element_type=F32)
    o_ref[...] = (a * jax.nn.sigmoid(a) * b).astype(o_ref.dtype)


def swiglu_up(x, w_in, tm, fc):
    rows, d = x.shape
    f = w_in.shape[1] // 2
    nf = f // fc
    return pl.pallas_call(
        _swiglu_up_kernel,
        grid=(nf, rows // tm),
        in_specs=[pl.BlockSpec((tm, d), lambda fi, i: (i, 0)),
                  pl.BlockSpec((d, fc), lambda fi, i: (0, fi)),
                  pl.BlockSpec((d, fc), lambda fi, i: (0, nf + fi))],
        out_specs=pl.BlockSpec((tm, fc), lambda fi, i: (i, fi)),
        out_shape=jax.ShapeDtypeStruct((rows, f), BF16),
        compiler_params=_cparams(("parallel", "parallel")),
        name="swiglu_up",
    )(x, w_in, w_in)


def _combine_ln_kernel(slot_ref, y_ref, w_ref, sh_ref, x_ref, g_ref, b_ref, o_ref, buf, sem, *, tt):
    base = pl.program_id(0) * tt * TOP_K

    def issue(n, carry):
        for k in range(TOP_K):
            pltpu.make_async_copy(y_ref.at[pl.ds(slot_ref[base + n * TOP_K + k], 1)],
                                  buf.at[pl.ds(k * tt + n, 1)], sem).start()
        return carry

    lax.fori_loop(0, tt, issue, 0)
    pltpu.make_async_copy(y_ref.at[pl.ds(0, tt * TOP_K)], buf, sem).wait()
    ff = sh_ref[...]
    for k in range(TOP_K):
        ff = ff + w_ref[:, k:k + 1] * buf[k * tt:(k + 1) * tt, :]
    o_ref[...] = _layer_norm(DN_ALPHA * x_ref[...] + ff, g_ref[...], b_ref[...])


def combine_ln(y, slots, gates, shared, x1, ln_g, ln_b, tt):
    rows, d = x1.shape
    vec = pl.BlockSpec((1, d), lambda i, s: (0, 0))
    tile = pl.BlockSpec((tt, d), lambda i, s: (i, 0))
    grid_spec = pltpu.PrefetchScalarGridSpec(
        num_scalar_prefetch=1,
        grid=(rows // tt,),
        in_specs=[pl.BlockSpec(memory_space=pl.ANY), pl.BlockSpec((tt, LANE), lambda i, s: (i, 0)),
                  tile, tile, vec, vec],
        out_specs=tile,
        scratch_shapes=[pltpu.VMEM((tt * TOP_K, d), F32), pltpu.SemaphoreType.DMA(())],
    )
    return pl.pallas_call(
        functools.partial(_combine_ln_kernel, tt=tt),
        grid_spec=grid_spec,
        out_shape=jax.ShapeDtypeStruct((rows, d), F32),
        compiler_params=_cparams(("arbitrary",)),
        name="combine_ln",
    )(slots, y, gates, shared, x1, ln_g.reshape(1, d), ln_b.reshape(1, d))


def _largest_tile(n, candidates):
    for c in candidates:
        if n % c == 0:
            return c
    raise ValueError(f"no tile in {candidates} divides {n}")


def moe(x1, x1b, w_router, router_bias, w_exp_in, w_exp_out, w_sh_in, w_sh_out, ln_g, ln_b):
    rows, d = x1.shape
    f = w_exp_out.shape[1]
    tm = 256
    eidx_p, wgt_p, rank_p, cnt = router(x1b, w_router, router_bias, _largest_tile(rows, (256, 128, 64, 32, 16)))
    eidx, rank = eidx_p[:, :TOP_K], rank_p[:, :TOP_K]
    counts = cnt[0, :N_EXP]
    padded = (counts + tm - 1) // tm * tm
    ends = jnp.cumsum(padded)
    offs = ends - padded
    n_tiles = (rows * TOP_K + N_EXP * (tm - 1) + tm - 1) // tm
    n_slots = n_tiles * tm
    n_used = (ends[-1] // tm).astype(I32).reshape(1)
    tile_start = jnp.arange(n_tiles, dtype=I32) * tm
    tile_expert = jnp.minimum(jnp.sum(ends[None, :] <= tile_start[:, None], axis=1), N_EXP - 1).astype(I32)
    slot = (offs[eidx] + rank).astype(I32).reshape(-1)
    token = jnp.repeat(jnp.arange(rows, dtype=I32), TOP_K)
    token_of_slot = jnp.zeros((n_slots,), I32).at[slot].set(token)

    xs = gather_rows(x1, token_of_slot, tm, BF16)
    hs = expert_up(xs, w_exp_in, tile_expert, n_used, tm, _largest_tile(f, (512, 256, 128)))
    ys = expert_down(hs, w_exp_out, tile_expert, n_used, tm, _largest_tile(d, (2048, 1024, 512, 256)))
    t_sh = _largest_tile(rows, (512, 256, 128, 64, 32, 16))
    h_sh = swiglu_up(x1b, w_sh_in, t_sh, _largest_tile(w_sh_in.shape[1] // 2, (512, 256, 128)))
    y_sh = matmul(h_sh, w_sh_out, t_sh, _largest_tile(d, (512, 256)))
    return combine_ln(ys, slot, wgt_p, y_sh, x1, ln_g, ln_b, _largest_tile(rows, (64, 32, 16, 8)))


def kernel(x_prompt, mem_prompt, x_sample, cache_k, cache_v, cache_idx_k, state_hgrn, cache_mem_k, cache_mem_v,
           page_table, w_in, w_idx_qb, idx_ln_g, idx_ln_b, hgrn_lb_logits, hgrn_norm_g, w_mem_kv, w_proj_a,
           w_proj_b, w_proj_m, w_o, ln1_g, ln1_b, w_router, router_bias, w_exp_in, w_exp_out, w_sh_in,
           w_sh_out, ln2_g, ln2_b):
    assert x_prompt.shape[0] == 1 and w_in.shape[0] == DEPTH
    _, L, D = x_prompt.shape
    B, T, _ = x_sample.shape
    RS = B * T
    R = L + RS
    page = cache_k.shape[2]
    n_pages = page_table.shape[1]
    past = n_pages * page
    n_mem = mem_prompt.shape[1]

    x = jnp.concatenate([x_prompt[0], x_sample.reshape(RS, D)], axis=0)
    n_a = C_WIDX + H_IDX
    w_pad = jnp.concatenate([w_in[0][:, :n_a].astype(BF16), jnp.zeros((D, LANE - H_IDX), BF16),
                             w_in[0][:, n_a:].astype(BF16)], axis=1)
    t_row = _largest_tile(R, (512, 256, 128, 64, 32, 16))
    z = matmul(x.astype(BF16), w_pad, t_row, _largest_tile(w_pad.shape[1], (640, 512, 256, 128)))
    pos = jnp.concatenate([jnp.arange(L, dtype=I32), past + jnp.arange(RS, dtype=I32) % T])
    tables = _rope_tables(pos)
    t_prep = _largest_tile(R, (256, 128))
    q_rot, k_rot, k_rot_b, v, v_t, k_idx, k_idx_b = prep_qkv(z, tables, idx_ln_g[0], idx_ln_b[0], t_prep)
    q_idx = qidx_proj(z, w_idx_qb[0].astype(BF16), tables, t_prep)
    w_idx = z[:, C_WIDX:C_WIDX + H_IDX]

    kc_p = _largest_tile(L, (1024, 512, 256, 128))
    scores_p = idx_scores_prompt(q_idx, (w_idx * IDX_W_SCALE).T, k_idx_b, L, LANE, kc_p, min(kc_p, 256))
    bias_p = select_bias(scores_p, jnp.arange(L, dtype=I32), min(TOPK_MAX, L // 4), LANE, min(kc_p, 512))
    o_a_p = attn_prompt(q_rot, k_rot_b, v_t, bias_p, L, _largest_tile(L, (256, 128)),
                        _largest_tile(L, (512, 256, 128)))

    q_s = q_idx[:, L:, :].reshape(H_IDX, B, T, D_IDX).transpose(1, 2, 0, 3).reshape(B, T * H_IDX, D_IDX)
    pad_new = lambda a: jnp.zeros((B, page, a.shape[-1]), BF16).at[:, :T].set(
        a.astype(BF16).reshape(B, T, a.shape[-1]))
    pad_q = lambda a, nh, dh: jnp.zeros((B, nh, Q_ROWS, dh), BF16).at[:, :, :T].set(
        a.astype(BF16).reshape(B, T, nh, dh).transpose(0, 2, 1, 3))
    scores_s = idx_scores_sample(q_s, w_idx[L:].reshape(B, T * H_IDX, 1), cache_idx_k[0], pad_new(k_idx_b[L:]),
                                 page_table)
    width_s = past + page
    lim_s = past + jnp.arange(RS, dtype=I32) % T
    bias_s = select_bias(scores_s.reshape(RS, width_s).T, lim_s, min(TOPK_MAX, (past + T) // 4),
                         _largest_tile(RS, (128, 64, 32, 16)), LANE)
    bias_s = jnp.zeros((B, Q_ROWS, width_s), F32).at[:, :T].set(bias_s.T.reshape(B, T, width_s))
    pages = lambda c: c[0].reshape(c.shape[1], page * H_A, DH_A)
    o_a_s = attn_sample(pad_q(q_rot[L:], H_A, DH_A), pages(cache_k), pages(cache_v), pad_new(k_rot_b[L:]),
                        pad_new(v[L:]), bias_s, page_table, T)

    lb = jnp.cumsum(jax.nn.softmax(hgrn_lb_logits.astype(F32), axis=0), axis=0)[0]
    o_b_p, s_p = hgrn_prompt(z, L, lb, hgrn_norm_g[0], _largest_tile(L, (512, 256, 128)), LANE, 32)
    o_b_s, s_s = hgrn_sample(z[L:, C_FB:C_QM], state_hgrn[0], lb, hgrn_norm_g[0], T, SUBLANE // T)

    mkv = matmul(mem_prompt[0].astype(BF16), w_mem_kv[0].astype(BF16),
                 _largest_tile(n_mem, (256, 128, 64, 32, 16)), _largest_tile(2 * W_M, (512,)))
    mk, mv = mkv[:, :W_M], mkv[:, W_M:]
    o_m_p = mem_attn_prompt(z, L, mk.astype(BF16), mv.astype(BF16), _largest_tile(L, (512, 256, 128)))
    qm_bd = jnp.einsum("btgd,hg->bhtgd", z[L:, C_QM:C_QM + W_M].astype(BF16).reshape(B, T, H_M, DH_M),
                       jnp.eye(H_M, dtype=BF16)).reshape(B, H_M * T, W_M)
    o_m_s = mem_attn_sample(qm_bd, cache_mem_k[0].reshape(B, n_mem, W_M), cache_mem_v[0].reshape(B, n_mem, W_M), T)

    o_a = jnp.concatenate([o_a_p, o_a_s.reshape(RS, W_A).astype(BF16)], axis=0)
    o_b = jnp.concatenate([o_b_p, o_b_s.astype(BF16)], axis=0)
    o_m = jnp.concatenate([o_m_p, o_m_s.reshape(RS, W_M).astype(BF16)], axis=0)
    h = gated_proj(o_a, o_b, o_m, w_proj_a[0].astype(BF16), w_proj_b[0].astype(BF16), w_proj_m[0].astype(BF16),
                   z, D, t_row, 256)
    x1, x1b = out_proj_ln(h, w_o[0].astype(BF16), x, ln1_g[0], ln1_b[0], t_prep, _largest_tile(D, (512, 256)))
    y = moe(x1, x1b, w_router[0], router_bias[0], w_exp_in[0], w_exp_out[0], w_sh_in[0], w_sh_out[0],
            ln2_g[0], ln2_b[0])

    return (y[:L].reshape(1, L, D), y[L:].reshape(B, T, D),
            k_rot[:L].reshape(1, 1, L, H_A, DH_A), v[:L].reshape(1, 1, L, H_A, DH_A),
            k_idx[:L].reshape(1, 1, L, D_IDX), s_p.reshape(1, 1, H_B, DK_B, DV_B),
            mk.reshape(1, 1, n_mem, H_M, DH_M), mv.reshape(1, 1, n_mem, H_M, DH_M),
            k_rot[L:].reshape(1, B, T, H_A, DH_A), v[L:].reshape(1, B, T, H_A, DH_A),
            k_idx[L:].reshape(1, B, T, D_IDX), s_s.reshape(1, B, H_B, DK_B, DV_B))
```

```python
import functools

import jax
import jax.numpy as jnp
import numpy as np
from jax import lax
from jax.experimental import pallas as pl
from jax.experimental.pallas import tpu as pltpu

F32 = jnp.float32
BF16 = jnp.bfloat16
I32 = jnp.int32

H_A = 16
DH_A = 128
W_A = H_A * DH_A
H_IDX = 32
D_IDX = 128
R_IDX = 512
IDX_ROPE = 64
IDX_W_SCALE = (H_IDX ** -0.5) * (D_IDX ** -0.5)
TOPK_MAX = 256
H_B = 8
DK_B = 128
DV_B = 128
W_B = H_B * DV_B
H_M = 4
DH_M = 256
W_M = H_M * DH_M
N_EXP = 64
TOP_K = 8
N_GROUP = 8
TOPK_GROUP = 4
ROUTED_SCALE = 2.5
ROPE_THETA = 10000.0
LN_EPS = 1e-5
N_BRANCH = 3
DEPTH = 1
DN_ALPHA = (2.0 * DEPTH) ** 0.25

LANE = 128
SUBLANE = 8
Q_ROWS = 16
NEG = -1e30
VMEM_LIMIT = 56 * 1024 * 1024

C_Q = 0
C_K = W_A
C_V = 2 * W_A
C_CIDX = 3 * W_A
C_KIDX = C_CIDX + R_IDX
C_WIDX = C_KIDX + D_IDX
C_FB = C_WIDX + LANE
C_IB = C_FB + H_B * DK_B
C_QB = C_IB + W_B
C_GB = C_QB + H_B * DK_B
C_QM = C_GB + W_B
C_GATE = C_QM + W_M

NT = (((1,), (1,)), ((), ()))


def _cparams(sem, vmem=VMEM_LIMIT):
    return pltpu.CompilerParams(dimension_semantics=sem, vmem_limit_bytes=vmem)


def _mm_kernel(x_ref, w_ref, o_ref):
    o_ref[...] = jnp.dot(x_ref[...], w_ref[...].astype(BF16),
                         preferred_element_type=F32).astype(o_ref.dtype)


def matmul(x, w, tm, tn, out_dtype=F32):
    m, k = x.shape
    n = w.shape[1]
    return pl.pallas_call(
        _mm_kernel,
        grid=(m // tm, n // tn),
        in_specs=[pl.BlockSpec((tm, k), lambda i, j: (i, 0)),
                  pl.BlockSpec((k, tn), lambda i, j: (0, j))],
        out_specs=pl.BlockSpec((tm, tn), lambda i, j: (i, j)),
        out_shape=jax.ShapeDtypeStruct((m, n), out_dtype),
        compiler_params=_cparams(("parallel", "parallel")),
        name="matmul",
    )(x, w)


def _rope_tables(pos):
    posf = pos.astype(F32)[:, None]
    half = DH_A // 2
    inv = ROPE_THETA ** (-jnp.arange(half, dtype=F32) / half)
    ang = posf * inv[None, :]
    c, s = jnp.cos(ang), jnp.sin(ang)
    cos_a = jnp.concatenate([c, c], axis=-1)
    sin_a = jnp.concatenate([-s, s], axis=-1)
    half_i = IDX_ROPE // 2
    inv_i = ROPE_THETA ** (-jnp.arange(half_i, dtype=F32) / half_i)
    ang_i = posf * inv_i[None, :]
    ci, si = jnp.cos(ang_i), jnp.sin(ang_i)
    one = jnp.ones((pos.shape[0], D_IDX - IDX_ROPE), F32)
    zero_h = jnp.zeros_like(si)
    zero_r = jnp.zeros_like(one)
    cos_i = jnp.concatenate([ci, ci, one], axis=-1)
    sin_dn = jnp.concatenate([zero_h, si, zero_r], axis=-1)
    sin_up = jnp.concatenate([-si, zero_h, zero_r], axis=-1)
    return cos_a, sin_a, cos_i, sin_dn, sin_up


def _rope_full(x, cos, sin):
    return x * cos + pltpu.roll(x, DH_A // 2, 1) * sin


def _rope_idx(x, cos, sin_dn, sin_up):
    h = IDX_ROPE // 2
    return x * cos + pltpu.roll(x, h, 1) * sin_dn + pltpu.roll(x, D_IDX - h, 1) * sin_up


def _prep_kernel(q_ref, k_ref, v_ref, ki_ref, cos_ref, sin_ref, cosi_ref, sdn_ref, sup_ref, g_ref, b_ref,
                 qo_ref, ko_ref, kb_ref, vo_ref, vt_ref, kio_ref, kib_ref):
    cos = cos_ref[...]
    sin = sin_ref[...]
    for h in range(H_A):
        sl = slice(h * DH_A, (h + 1) * DH_A)
        qo_ref[:, sl] = _rope_full(q_ref[:, sl], cos, sin).astype(qo_ref.dtype)
        kr = _rope_full(k_ref[:, sl], cos, sin)
        ko_ref[:, sl] = kr
        kb_ref[:, sl] = kr.astype(kb_ref.dtype)
        v = v_ref[:, sl]
        vo_ref[:, sl] = v
        vt_ref[sl, :] = v.T.astype(vt_ref.dtype)
    x = ki_ref[...]
    mu = jnp.mean(x, axis=-1, keepdims=True)
    var = jnp.mean(jnp.square(x - mu), axis=-1, keepdims=True)
    xn = (x - mu) * lax.rsqrt(var + LN_EPS) * g_ref[...] + b_ref[...]
    ki = _rope_idx(xn, cosi_ref[...], sdn_ref[...], sup_ref[...])
    kio_ref[...] = ki
    kib_ref[...] = ki.astype(kib_ref.dtype)


def prep_qkv(z, tables, ln_g, ln_b, tr):
    rows = z.shape[0]
    cos_a, sin_a, cos_i, sin_dn, sin_up = tables
    row_blk = lambda w, c: pl.BlockSpec((tr, w), lambda i: (i, c))
    tab = pl.BlockSpec((tr, LANE), lambda i: (i, 0))
    vec = pl.BlockSpec((1, LANE), lambda i: (0, 0))
    wide = pl.BlockSpec((tr, W_A), lambda i: (i, 0))
    narrow = pl.BlockSpec((tr, D_IDX), lambda i: (i, 0))
    return pl.pallas_call(
        _prep_kernel,
        grid=(rows // tr,),
        in_specs=[row_blk(W_A, C_Q // W_A), row_blk(W_A, C_K // W_A), row_blk(W_A, C_V // W_A),
                  row_blk(D_IDX, C_KIDX // D_IDX), tab, tab, tab, tab, tab, vec, vec],
        out_specs=[wide, wide, wide, wide, pl.BlockSpec((W_A, tr), lambda i: (0, i)), narrow, narrow],
        out_shape=[jax.ShapeDtypeStruct((rows, W_A), BF16),
                   jax.ShapeDtypeStruct((rows, W_A), F32),
                   jax.ShapeDtypeStruct((rows, W_A), BF16),
                   jax.ShapeDtypeStruct((rows, W_A), F32),
                   jax.ShapeDtypeStruct((W_A, rows), BF16),
                   jax.ShapeDtypeStruct((rows, D_IDX), F32),
                   jax.ShapeDtypeStruct((rows, D_IDX), BF16)],
        compiler_params=_cparams(("parallel",)),
        name="prep_qkv",
    )(z, z, z, z, cos_a, sin_a, cos_i, sin_dn, sin_up, ln_g.reshape(1, D_IDX), ln_b.reshape(1, D_IDX))


def _qidx_kernel(c_ref, w_ref, cosi_ref, sdn_ref, sup_ref, o_ref):
    acc = jnp.dot(c_ref[...].astype(BF16), w_ref[...], preferred_element_type=F32)
    cos, sdn, sup = cosi_ref[...], sdn_ref[...], sup_ref[...]
    for h in range(H_IDX):
        o_ref[h] = _rope_idx(acc[:, h * D_IDX:(h + 1) * D_IDX], cos, sdn, sup).astype(o_ref.dtype)


def qidx_proj(z, w_qb, tables, tr):
    rows = z.shape[0]
    _, _, cos_i, sin_dn, sin_up = tables
    tab = pl.BlockSpec((tr, LANE), lambda i: (i, 0))
    return pl.pallas_call(
        _qidx_kernel,
        grid=(rows // tr,),
        in_specs=[pl.BlockSpec((tr, R_IDX), lambda i: (i, C_CIDX // R_IDX)),
                  pl.BlockSpec((R_IDX, H_IDX * D_IDX), lambda i: (0, 0)),
                  tab, tab, tab],
        out_specs=pl.BlockSpec((H_IDX, tr, D_IDX), lambda i: (0, i, 0)),
        out_shape=jax.ShapeDtypeStruct((H_IDX, rows, D_IDX), BF16),
        compiler_params=_cparams(("parallel",)),
        name="qidx_proj",
    )(z, w_qb, cos_i, sin_dn, sin_up)


def _idx_score_prompt_kernel(q_ref, w_ref, k_ref, o_ref, *, tq, kc, ks):
    i, j = pl.program_id(0), pl.program_id(1)

    @pl.when(j * kc <= i * tq + tq - 1)
    def _():
        for u in range(kc // ks):
            k = k_ref[u * ks:(u + 1) * ks, :]
            acc = jnp.zeros((ks, tq), F32)
            for h in range(H_IDX):
                s = lax.dot_general(k, q_ref[h], NT, preferred_element_type=F32)
                acc = acc + jnp.maximum(s, 0.0) * w_ref[h:h + 1, :]
            o_ref[u * ks:(u + 1) * ks, :] = acc

    @pl.when(j * kc > i * tq + tq - 1)
    def _():
        o_ref[...] = jnp.zeros(o_ref.shape, F32)


def idx_scores_prompt(q_idx, w_t, k_idx_b, L, tq, kc, ks):
    def kmap(i, j):
        return (jnp.minimum(j, (i * tq + tq - 1) // kc), 0)

    return pl.pallas_call(
        functools.partial(_idx_score_prompt_kernel, tq=tq, kc=kc, ks=ks),
        grid=(L // tq, L // kc),
        in_specs=[pl.BlockSpec((H_IDX, tq, D_IDX), lambda i, j: (0, i, 0)),
                  pl.BlockSpec((H_IDX, tq), lambda i, j: (0, i)),
                  pl.BlockSpec((kc, D_IDX), kmap)],
        out_specs=pl.BlockSpec((kc, tq), lambda i, j: (j, i)),
        out_shape=jax.ShapeDtypeStruct((L, L), F32),
        compiler_params=_cparams(("parallel", "arbitrary")),
        name="idx_scores_prompt",
    )(q_idx, w_t, k_idx_b)


def _idx_score_sample_kernel(pt_ref, q_ref, w_ref, *refs, n_pages, n_new):
    page_refs = refs[:n_pages]
    knew_ref = refs[n_pages]
    o_ref = refs[n_pages + 1]
    q = q_ref[...]
    w = w_ref[...] * IDX_W_SCALE
    page = page_refs[0].shape[0]

    def score(k):
        s = lax.dot_general(q, k, NT, preferred_element_type=F32)
        s = jnp.maximum(s, 0.0) * w
        return jnp.sum(s.reshape(n_new, H_IDX, k.shape[0]), axis=1)

    for p in range(n_pages):
        o_ref[:, p * page:(p + 1) * page] = score(page_refs[p][...].astype(BF16))
    o_ref[:, n_pages * page:] = score(knew_ref[...])


def idx_scores_sample(q_s, w_s, cache_idx, knew_idx, page_table):
    nb, rows, _ = q_s.shape
    n_new = rows // H_IDX
    n_pages = page_table.shape[1]
    page = cache_idx.shape[1]
    page_specs = [pl.BlockSpec((None, page, D_IDX), functools.partial(lambda b, pt, p: (pt[b, p], 0, 0), p=p))
                  for p in range(n_pages)]
    grid_spec = pltpu.PrefetchScalarGridSpec(
        num_scalar_prefetch=1,
        grid=(nb,),
        in_specs=[pl.BlockSpec((None, rows, D_IDX), lambda b, pt: (b, 0, 0)),
                  pl.BlockSpec((None, rows, 1), lambda b, pt: (b, 0, 0))] + page_specs +
                 [pl.BlockSpec((None, page, D_IDX), lambda b, pt: (b, 0, 0))],
        out_specs=pl.BlockSpec((None, n_new, (n_pages + 1) * page), lambda b, pt: (b, 0, 0)),
    )
    return pl.pallas_call(
        functools.partial(_idx_score_sample_kernel, n_pages=n_pages, n_new=n_new),
        grid_spec=grid_spec,
        out_shape=jax.ShapeDtypeStruct((nb, n_new, (n_pages + 1) * page), F32),
        compiler_params=_cparams(("arbitrary",)),
        name="idx_scores_sample",
    )(page_table, q_s, w_s, *([cache_idx] * n_pages), knew_idx)


def _key_to_float(key):
    bits = key ^ (jnp.right_shift(key, 31) & jnp.int32(0x7FFFFFFF))
    return lax.bitcast_convert_type(bits, F32)


def _select_kernel(nch_ref, lim_ref, s_ref, o_ref, m_ref, *, n_sel, kc):
    width, tq = s_ref.shape
    n_chunks = nch_ref[pl.program_id(0)]
    lim = lim_ref[...]
    kq = jnp.minimum(lim + 1, n_sel).astype(F32)

    def mask_chunk(c, carry):
        rows = pl.ds(pl.multiple_of(c * kc, kc), kc)
        row = c * kc + lax.broadcasted_iota(I32, (kc, tq), 0)
        m_ref[rows, :] = jnp.where(row <= lim, s_ref[rows, :], -jnp.inf)
        return carry

    lax.fori_loop(0, n_chunks, mask_chunk, 0)

    acc_rows = min(kc, 8 * SUBLANE)

    def count_ge(thr):
        def body(c, acc):
            blk = m_ref[pl.ds(pl.multiple_of(c * kc, kc), kc), :]
            hit = jnp.where(blk >= thr, 1.0, 0.0)
            return acc + jnp.sum(hit.reshape(kc // acc_rows, acc_rows, tq), axis=0)

        acc = lax.fori_loop(0, n_chunks, body, jnp.zeros((acc_rows, tq), F32))
        return jnp.sum(acc, axis=0, keepdims=True)

    def bit_step(it, key):
        cand = key + jnp.left_shift(jnp.int32(1), 31 - it)
        ok = count_ge(_key_to_float(cand)) >= kq
        return jnp.where(ok, cand, key)

    key = lax.fori_loop(0, 32, bit_step, jnp.full((1, tq), jnp.iinfo(jnp.int32).min, I32))
    thr = _key_to_float(key)

    def out_chunk(c, carry):
        rows = pl.ds(pl.multiple_of(c * kc, kc), kc)
        o_ref[rows, :] = jnp.where(m_ref[rows, :] >= thr, 0.0, NEG)
        return carry

    lax.fori_loop(0, n_chunks, out_chunk, 0)

    def fill_chunk(c, carry):
        o_ref[pl.ds(pl.multiple_of(c * kc, kc), kc), :] = jnp.full((kc, tq), NEG, F32)
        return carry

    lax.fori_loop(n_chunks, width // kc, fill_chunk, 0)


def select_bias(scores_t, lim, n_sel, tq, kc):
    width, n_q = scores_t.shape
    n_tiles = n_q // tq
    nch = (jnp.max(lim.reshape(n_tiles, tq), axis=1) + kc) // kc
    grid_spec = pltpu.PrefetchScalarGridSpec(
        num_scalar_prefetch=1,
        grid=(n_tiles,),
        in_specs=[pl.BlockSpec((1, tq), lambda i, n: (0, i)),
                  pl.BlockSpec((width, tq), lambda i, n: (0, i))],
        out_specs=pl.BlockSpec((width, tq), lambda i, n: (0, i)),
        scratch_shapes=[pltpu.VMEM((width, tq), F32)],
    )
    return pl.pallas_call(
        functools.partial(_select_kernel, n_sel=n_sel, kc=kc),
        grid_spec=grid_spec,
        out_shape=jax.ShapeDtypeStruct((width, n_q), F32),
        compiler_params=_cparams(("arbitrary",)),
        name="select_bias",
    )(nch.astype(I32), lim.reshape(1, n_q).astype(I32), scores_t)


def _attn_prompt_kernel(q_ref, k_ref, vt_ref, b_ref, o_ref, m_sc, l_sc, acc_sc, *, tq, tk):
    i, j = pl.program_id(0), pl.program_id(1)
    last = (i * tq + tq - 1) // tk
    scale = DH_A ** -0.5

    @pl.when(j == 0)
    def _():
        m_sc[...] = jnp.full(m_sc.shape, NEG, F32)
        l_sc[...] = jnp.zeros(l_sc.shape, F32)
        acc_sc[...] = jnp.zeros(acc_sc.shape, F32)

    @pl.when(j <= last)
    def _():
        for h in range(H_A):
            sl = slice(h * DH_A, (h + 1) * DH_A)
            q_h = q_ref[:, sl]
            m, l, acc = m_sc[h], l_sc[h], acc_sc[h]
            for u in range(tk // LANE):
                keys = slice(u * LANE, (u + 1) * LANE)
                s = lax.dot_general(k_ref[keys, sl], q_h, NT, preferred_element_type=F32) * scale + b_ref[keys, :]
                m_new = jnp.maximum(m, jnp.max(s, axis=0, keepdims=True))
                alpha = jnp.exp(m - m_new)
                p = jnp.exp(s - m_new)
                l = alpha * l + jnp.sum(p, axis=0, keepdims=True)
                acc = alpha * acc + jnp.dot(vt_ref[sl, keys], p.astype(BF16), preferred_element_type=F32)
                m = m_new
            m_sc[h], l_sc[h], acc_sc[h] = m, l, acc

    @pl.when(j == last)
    def _():
        for h in range(H_A):
            o_ref[:, h * DH_A:(h + 1) * DH_A] = (acc_sc[h] / l_sc[h]).T.astype(o_ref.dtype)


def attn_prompt(q, k, v_t, bias_t, L, tq, tk):
    def jmap(i, j):
        return jnp.minimum(j, (i * tq + tq - 1) // tk)

    return pl.pallas_call(
        functools.partial(_attn_prompt_kernel, tq=tq, tk=tk),
        grid=(L // tq, L // tk),
        in_specs=[pl.BlockSpec((tq, W_A), lambda i, j: (i, 0)),
                  pl.BlockSpec((tk, W_A), lambda i, j: (jmap(i, j), 0)),
                  pl.BlockSpec((W_A, tk), lambda i, j: (0, jmap(i, j))),
                  pl.BlockSpec((tk, tq), lambda i, j: (jmap(i, j), i))],
        out_specs=pl.BlockSpec((tq, W_A), lambda i, j: (i, 0)),
        out_shape=jax.ShapeDtypeStruct((L, W_A), BF16),
        scratch_shapes=[pltpu.VMEM((H_A, 1, tq), F32), pltpu.VMEM((H_A, 1, tq), F32),
                        pltpu.VMEM((H_A, DH_A, tq), F32)],
        compiler_params=_cparams(("parallel", "arbitrary")),
        name="attn_prompt",
    )(q, k, v_t, bias_t)


def _attn_sample_kernel(pt_ref, q_ref, *refs, n_steps, group, n_new):
    kc_refs, vc_refs = refs[:group], refs[group:2 * group]
    kn_ref, vn_ref, b_ref, bn_ref, o_ref, m_sc, l_sc, acc_sc = refs[2 * group:]
    p = pl.program_id(1)
    scale = DH_A ** -0.5

    @pl.when(p == 0)
    def _():
        m_sc[...] = jnp.full(m_sc.shape, NEG, F32)
        l_sc[...] = jnp.zeros(l_sc.shape, F32)
        acc_sc[...] = jnp.zeros(acc_sc.shape, F32)

    def update(keys, values, bias):
        s = jnp.stack([lax.dot_general(q_ref[h], keys(h), NT, preferred_element_type=F32) for h in range(H_A)])
        s = s * scale + bias[None]
        m_prev = m_sc[...]
        m_new = jnp.maximum(m_prev, jnp.max(s, axis=-1, keepdims=True))
        alpha = jnp.exp(m_prev - m_new)
        pr = jnp.exp(s - m_new)
        l_sc[...] = alpha * l_sc[...] + jnp.sum(pr, axis=-1, keepdims=True)
        pb = pr.astype(BF16)
        pv = jnp.stack([jnp.dot(pb[h], values(h), preferred_element_type=F32) for h in range(H_A)])
        acc_sc[...] = alpha * acc_sc[...] + pv
        m_sc[...] = m_new

    @pl.when(p < n_steps)
    def _():
        page = kc_refs[0].shape[0] // H_A
        for g in range(group):
            update(lambda h, g=g: kc_refs[g][pl.ds(h, page, stride=H_A), :].astype(BF16),
                   lambda h, g=g: vc_refs[g][pl.ds(h, page, stride=H_A), :].astype(BF16),
                   b_ref[:, g * page:(g + 1) * page])

    @pl.when(p == n_steps)
    def _():
        update(lambda h: kn_ref[:, h * DH_A:(h + 1) * DH_A], lambda h: vn_ref[:, h * DH_A:(h + 1) * DH_A],
               bn_ref[...])
        out = acc_sc[...] / l_sc[...]
        for h in range(H_A):
            o_ref[:, h * DH_A:(h + 1) * DH_A] = out[h, :n_new]


def attn_sample(q_h, cache_k, cache_v, knew, vnew, bias_past, bias_new, page_table, n_new):
    nb = q_h.shape[0]
    n_pages = page_table.shape[1]
    page = cache_k.shape[1] // H_A
    group = _largest_tile(n_pages, (4, 2, 1))
    n_steps = n_pages // group

    def cmap(g):
        return lambda b, p, pt: (pt[b, jnp.minimum(p, n_steps - 1) * group + g], 0, 0)

    new_spec = pl.BlockSpec((None, page, W_A), lambda b, p, pt: (b, 0, 0))
    cache_specs = [pl.BlockSpec((None, page * H_A, DH_A), cmap(g)) for g in range(group)]
    grid_spec = pltpu.PrefetchScalarGridSpec(
        num_scalar_prefetch=1,
        grid=(nb, n_steps + 1),
        in_specs=[pl.BlockSpec((None, H_A, Q_ROWS, DH_A), lambda b, p, pt: (b, 0, 0, 0))] +
                 cache_specs + cache_specs + [new_spec, new_spec,
                  pl.BlockSpec((None, Q_ROWS, group * page), lambda b, p, pt: (b, 0, jnp.minimum(p, n_steps - 1))),
                  pl.BlockSpec((None, Q_ROWS, page), lambda b, p, pt: (b, 0, 0))],
        out_specs=pl.BlockSpec((None, n_new, W_A), lambda b, p, pt: (b, 0, 0)),
        scratch_shapes=[pltpu.VMEM((H_A, Q_ROWS, 1), F32), pltpu.VMEM((H_A, Q_ROWS, 1), F32),
                        pltpu.VMEM((H_A, Q_ROWS, DH_A), F32)],
    )
    return pl.pallas_call(
        functools.partial(_attn_sample_kernel, n_steps=n_steps, group=group, n_new=n_new),
        grid_spec=grid_spec,
        out_shape=jax.ShapeDtypeStruct((nb, n_new, W_A), F32),
        compiler_params=_cparams(("parallel", "arbitrary")),
        name="attn_sample",
    )(page_table, q_h, *([cache_k] * group), *([cache_v] * group), knew, vnew, bias_past, bias_new)


def _hgrn_chunk(fr, ic, qc, lb, st, sub, row_valid=None):
    C = fr.shape[0]
    lf = jnp.log(lb + (1.0 - lb) * jax.nn.sigmoid(fr))
    kk = (1.0 - lb) * jax.nn.sigmoid(-fr)
    if row_valid is not None:
        lf = jnp.where(row_valid, lf, 0.0)
        kk = jnp.where(row_valid, kk, 0.0)
    r = lax.broadcasted_iota(I32, (C, C), 0)
    c = lax.broadcasted_iota(I32, (C, C), 1)
    tri = jnp.where(r >= c, 1.0, 0.0).astype(F32)
    G = jnp.dot(tri, lf, preferred_element_type=F32, precision=lax.Precision.HIGHEST)
    o_inter = lax.dot_general((qc * jnp.exp(G)).astype(BF16), st.astype(BF16), NT, preferred_element_type=F32)
    ic_b = ic.astype(BF16)
    t3 = lax.broadcasted_iota(I32, (sub, sub, DK_B), 0)
    s3 = lax.broadcasted_iota(I32, (sub, sub, DK_B), 1)
    pieces = []
    for a in range(C // sub):
        r0 = a * sub
        Gs, ks, qs, is_ = G[r0:r0 + sub], kk[r0:r0 + sub], qc[r0:r0 + sub], ic[r0:r0 + sub]
        dec = jnp.exp(jnp.where(t3 >= s3, Gs[:, None, :] - Gs[None, :, :], -jnp.inf))
        a3 = jnp.sum(dec * ks[None, :, :] * qs[:, None, :], axis=-1, keepdims=True)
        o_a = jnp.sum(a3 * is_[None, :, :], axis=1)
        if a > 0:
            gb = G[r0 - 1:r0]
            qa = (qs * jnp.exp(Gs - gb)).astype(BF16)
            kb = (kk[:r0] * jnp.exp(gb - G[:r0])).astype(BF16)
            A = lax.dot_general(qa, kb, NT, preferred_element_type=F32)
            o_a = o_a + jnp.dot(A.astype(BF16), ic_b[:r0], preferred_element_type=F32)
        pieces.append(o_a)
    o = o_inter + (pieces[0] if len(pieces) == 1 else jnp.concatenate(pieces, axis=0))
    g_last = G[C - 1:C]
    kd = (kk * jnp.exp(g_last - G)).astype(BF16)
    st_new = st * jnp.exp(g_last) + lax.dot_general(ic_b, kd, (((0,), (0,)), ((), ())),
                                                    preferred_element_type=F32)
    return o, st_new


def _hgrn_finish(o, gate, norm_g):
    o = o * lax.rsqrt(jnp.mean(jnp.square(o), axis=-1, keepdims=True) + LN_EPS)
    return o * norm_g * (gate * jax.nn.sigmoid(gate))


def _hgrn_prompt_kernel(f_ref, i_ref, q_ref, g_ref, lb_ref, ng_ref, o_ref, s_ref, st_sc, *, chunk, sub):
    j = pl.program_id(1)

    @pl.when(j == 0)
    def _():
        st_sc[...] = jnp.zeros(st_sc.shape, F32)

    lb = lb_ref[...]
    ng = ng_ref[...]

    def body(c, carry):
        rows = pl.ds(pl.multiple_of(c * chunk, chunk), chunk)
        o, st_new = _hgrn_chunk(f_ref[rows, :], i_ref[rows, :], q_ref[rows, :], lb, st_sc[...], sub)
        st_sc[...] = st_new
        o_ref[rows, :] = _hgrn_finish(o, g_ref[rows, :], ng).astype(o_ref.dtype)
        return carry

    lax.fori_loop(0, f_ref.shape[0] // chunk, body, 0)

    @pl.when(j == pl.num_programs(1) - 1)
    def _():
        s_ref[...] = st_sc[...].T


def hgrn_prompt(z, n_rows, lb, norm_g, tb, chunk, sub):
    blk = lambda c0: pl.BlockSpec((tb, DK_B), lambda h, j, c0=c0: (j, c0 // DK_B + h))
    vec = pl.BlockSpec((1, DK_B), lambda h, j: (0, h))
    return pl.pallas_call(
        functools.partial(_hgrn_prompt_kernel, chunk=chunk, sub=sub),
        grid=(H_B, n_rows // tb),
        in_specs=[blk(C_FB), blk(C_IB), blk(C_QB), blk(C_GB), vec,
                  pl.BlockSpec((1, DV_B), lambda h, j: (0, 0))],
        out_specs=[pl.BlockSpec((tb, DV_B), lambda h, j: (j, h)),
                   pl.BlockSpec((None, DK_B, DV_B), lambda h, j: (h, 0, 0))],
        out_shape=[jax.ShapeDtypeStruct((n_rows, W_B), BF16),
                   jax.ShapeDtypeStruct((H_B, DK_B, DV_B), F32)],
        scratch_shapes=[pltpu.VMEM((DV_B, DK_B), F32)],
        compiler_params=_cparams(("parallel", "arbitrary")),
        name="hgrn_prompt",
    )(z, z, z, z, lb.reshape(1, H_B * DK_B), norm_g.reshape(1, DV_B))


def _hgrn_sample_kernel(x_ref, s_ref, lb_ref, ng_ref, o_ref, so_ref, *, n_new, seqs):
    rows = seqs * n_new
    rid = lax.broadcasted_iota(I32, (rows, DK_B), 0)
    ng = ng_ref[...]
    for h in range(H_B):
        col = lambda g: slice(g * W_B + h * DK_B, g * W_B + (h + 1) * DK_B)
        fr, ic, qc, gate = x_ref[:, col(0)], x_ref[:, col(1)], x_ref[:, col(2)], x_ref[:, col(3)]
        lb = lb_ref[:, h * DK_B:(h + 1) * DK_B]
        o = jnp.zeros((rows, DV_B), F32)
        for b in range(seqs):
            valid = (rid >= b * n_new) & (rid < (b + 1) * n_new)
            ob, st_new = _hgrn_chunk(fr, ic, qc, lb, s_ref[b, h].T, rows, row_valid=valid)
            so_ref[b, h] = st_new.T
            o = jnp.where(valid, ob, o)
        o_ref[:, h * DV_B:(h + 1) * DV_B] = _hgrn_finish(o, gate, ng).astype(o_ref.dtype)


def hgrn_sample(xs, state, lb, norm_g, n_new, seqs):
    nb = state.shape[0]
    rows = seqs * n_new
    return pl.pallas_call(
        functools.partial(_hgrn_sample_kernel, n_new=n_new, seqs=seqs),
        grid=(nb // seqs,),
        in_specs=[pl.BlockSpec((rows, 4 * W_B), lambda i: (i, 0)),
                  pl.BlockSpec((seqs, H_B, DK_B, DV_B), lambda i: (i, 0, 0, 0)),
                  pl.BlockSpec((1, H_B * DK_B), lambda i: (0, 0)),
                  pl.BlockSpec((1, DV_B), lambda i: (0, 0))],
        out_specs=[pl.BlockSpec((rows, W_B), lambda i: (i, 0)),
                   pl.BlockSpec((seqs, H_B, DK_B, DV_B), lambda i: (i, 0, 0, 0))],
        out_shape=[jax.ShapeDtypeStruct((nb * n_new, W_B), F32),
                   jax.ShapeDtypeStruct(state.shape, F32)],
        compiler_params=_cparams(("parallel",)),
        name="hgrn_sample",
    )(xs, state, lb.reshape(1, H_B * DK_B), norm_g.reshape(1, DV_B))


def _softmax_rows(s):
    m = jnp.max(s, axis=-1, keepdims=True)
    p = jnp.exp(s - m)
    return p / jnp.sum(p, axis=-1, keepdims=True)


def _mem_attn_prompt_kernel(*refs):
    q_refs, mk_ref, mv_ref, o_ref = refs[:H_M], refs[H_M], refs[H_M + 1], refs[H_M + 2]
    for h in range(H_M):
        sl = slice(h * DH_M, (h + 1) * DH_M)
        s = lax.dot_general(q_refs[h][...].astype(BF16), mk_ref[:, sl], NT,
                            preferred_element_type=F32) * (DH_M ** -0.5)
        p = _softmax_rows(s).astype(BF16)
        o_ref[:, sl] = jnp.dot(p, mv_ref[:, sl], preferred_element_type=F32).astype(o_ref.dtype)


def mem_attn_prompt(z, n_rows, mk, mv, tr):
    n_mem = mk.shape[0]
    q_specs = [pl.BlockSpec((tr, DH_M), functools.partial(lambda i, h: (i, C_QM // DH_M + h), h=h))
               for h in range(H_M)]
    kv = pl.BlockSpec((n_mem, W_M), lambda i: (0, 0))
    return pl.pallas_call(
        _mem_attn_prompt_kernel,
        grid=(n_rows // tr,),
        in_specs=q_specs + [kv, kv],
        out_specs=pl.BlockSpec((tr, W_M), lambda i: (i, 0)),
        out_shape=jax.ShapeDtypeStruct((n_rows, W_M), BF16),
        compiler_params=_cparams(("parallel",)),
        name="mem_attn_prompt",
    )(*([z] * H_M), mk, mv)


def _mem_attn_sample_kernel(q_ref, k_ref, v_ref, o_ref, *, n_new):
    s = lax.dot_general(q_ref[...], k_ref[...].astype(BF16), NT,
                        preferred_element_type=F32) * (DH_M ** -0.5)
    p = _softmax_rows(s).astype(BF16)
    full = jnp.dot(p, v_ref[...].astype(BF16), preferred_element_type=F32)
    for h in range(H_M):
        o_ref[:, h * DH_M:(h + 1) * DH_M] = full[h * n_new:(h + 1) * n_new, h * DH_M:(h + 1) * DH_M]


def mem_attn_sample(q_bd, cache_mk, cache_mv, n_new):
    nb, rows, _ = q_bd.shape
    n_mem = cache_mk.shape[1]
    return pl.pallas_call(
        functools.partial(_mem_attn_sample_kernel, n_new=n_new),
        grid=(nb,),
        in_specs=[pl.BlockSpec((None, rows, W_M), lambda b: (b, 0, 0)),
                  pl.BlockSpec((None, n_mem, W_M), lambda b: (b, 0, 0)),
                  pl.BlockSpec((None, n_mem, W_M), lambda b: (b, 0, 0))],
        out_specs=pl.BlockSpec((None, n_new, W_M), lambda b: (b, 0, 0)),
        out_shape=jax.ShapeDtypeStruct((nb, n_new, W_M), F32),
        compiler_params=_cparams(("parallel",)),
        name="mem_attn_sample",
    )(q_bd, cache_mk, cache_mv)


def _gated_proj_kernel(oa_ref, ob_ref, om_ref, wa_ref, wb_ref, wm_ref, g0_ref, g1_ref, g2_ref, o_ref):
    def branch(x_ref, w_ref, g_ref):
        return jax.nn.sigmoid(g_ref[...]) * jnp.dot(x_ref[...], w_ref[...], preferred_element_type=F32)

    h = branch(oa_ref, wa_ref, g0_ref) + branch(ob_ref, wb_ref, g1_ref) + branch(om_ref, wm_ref, g2_ref)
    o_ref[...] = h.astype(o_ref.dtype)


def gated_proj(o_a, o_b, o_m, w_a, w_b, w_m, z, d_model, tm, tn):
    rows = o_a.shape[0]
    x_spec = lambda w: pl.BlockSpec((tm, w), lambda i, j: (i, 0))
    w_spec = lambda w: pl.BlockSpec((w, tn), lambda i, j: (0, j))
    g_spec = lambda g: pl.BlockSpec((tm, tn), lambda i, j, g=g: (i, (C_GATE + g * d_model) // tn + j))
    return pl.pallas_call(
        _gated_proj_kernel,
        grid=(rows // tm, d_model // tn),
        in_specs=[x_spec(W_A), x_spec(W_B), x_spec(W_M), w_spec(W_A), w_spec(W_B), w_spec(W_M),
                  g_spec(0), g_spec(1), g_spec(2)],
        out_specs=pl.BlockSpec((tm, tn), lambda i, j: (i, j)),
        out_shape=jax.ShapeDtypeStruct((rows, d_model), BF16),
        compiler_params=_cparams(("parallel", "parallel")),
        name="gated_proj",
    )(o_a, o_b, o_m, w_a, w_b, w_m, z, z, z)


def _layer_norm(x, g, b):
    mu = jnp.mean(x, axis=-1, keepdims=True)
    var = jnp.mean(jnp.square(x - mu), axis=-1, keepdims=True)
    return (x - mu) * lax.rsqrt(var + LN_EPS) * g + b


def _out_proj_ln_kernel(h_ref, w_ref, x_ref, g_ref, b_ref, o_ref, ob_ref, acc_sc):
    k = pl.program_id(1)

    @pl.when(k == 0)
    def _():
        acc_sc[...] = jnp.zeros(acc_sc.shape, F32)

    acc_sc[...] += jnp.dot(h_ref[...], w_ref[...], preferred_element_type=F32)

    @pl.when(k == pl.num_programs(1) - 1)
    def _():
        y = _layer_norm(DN_ALPHA * x_ref[...] + acc_sc[...], g_ref[...], b_ref[...])
        o_ref[...] = y
        ob_ref[...] = y.astype(ob_ref.dtype)


def out_proj_ln(h, w_o, x, ln_g, ln_b, tm, tk):
    rows, d = x.shape
    vec = pl.BlockSpec((1, d), lambda i, k: (0, 0))
    return pl.pallas_call(
        _out_proj_ln_kernel,
        grid=(rows // tm, d // tk),
        in_specs=[pl.BlockSpec((tm, tk), lambda i, k: (i, k)),
                  pl.BlockSpec((tk, d), lambda i, k: (k, 0)),
                  pl.BlockSpec((tm, d), lambda i, k: (i, 0)), vec, vec],
        out_specs=[pl.BlockSpec((tm, d), lambda i, k: (i, 0)),
                   pl.BlockSpec((tm, d), lambda i, k: (i, 0))],
        out_shape=[jax.ShapeDtypeStruct((rows, d), F32), jax.ShapeDtypeStruct((rows, d), BF16)],
        scratch_shapes=[pltpu.VMEM((tm, d), F32)],
        compiler_params=_cparams(("parallel", "arbitrary")),
        name="out_proj_ln",
    )(h, w_o, x, ln_g.reshape(1, d), ln_b.reshape(1, d))


def _router_kernel(x_ref, w_ref, b_ref, sel_ref, exp_ref, idx_ref, wgt_ref, rank_ref, cnt_ref, carry_sc):
    i = pl.program_id(0)

    @pl.when(i == 0)
    def _():
        carry_sc[...] = jnp.zeros(carry_sc.shape, F32)

    tr = x_ref.shape[0]
    per_group = N_EXP // N_GROUP
    lane = lax.broadcasted_iota(I32, (tr, LANE), 1)
    s = jax.nn.sigmoid(jnp.dot(x_ref[...], w_ref[...], preferred_element_type=F32))
    real = lane < N_EXP
    choice = jnp.where(real, s + b_ref[...], 0.0)
    hi = lax.Precision.HIGHEST

    def first_argmax(v):
        mx = jnp.max(v, axis=-1, keepdims=True)
        return mx, jnp.min(jnp.where(v == mx, lane, LANE), axis=-1, keepdims=True)

    members = [jnp.dot(choice, sel_ref[r], preferred_element_type=F32, precision=hi) for r in range(per_group)]
    m1 = functools.reduce(jnp.maximum, members)
    first = functools.reduce(jnp.minimum, [jnp.where(members[r] == m1, r, per_group) for r in range(per_group)])
    m2 = functools.reduce(jnp.maximum, [jnp.where(first == r, -jnp.inf, members[r]) for r in range(per_group)])
    cur = jnp.where(lane < N_GROUP, m1 + m2, -jnp.inf)
    gsel = jnp.zeros((tr, LANE), F32)
    for _ in range(TOPK_GROUP):
        _, gi = first_argmax(cur)
        hit = lane == gi
        gsel = jnp.where(hit, 1.0, gsel)
        cur = jnp.where(hit, -jnp.inf, cur)
    emask = jnp.dot(gsel, exp_ref[...], preferred_element_type=F32, precision=hi)
    masked = jnp.where((emask > 0.5) & real, choice, -jnp.inf)

    idx_out = jnp.zeros((tr, LANE), I32)
    wgt_out = jnp.zeros((tr, LANE), F32)
    onehot = jnp.zeros((tr, LANE), F32)
    picks = []
    for k in range(TOP_K):
        _, ei = first_argmax(masked)
        hit = lane == ei
        wk = jnp.sum(jnp.where(hit, s, 0.0), axis=-1, keepdims=True)
        masked = jnp.where(hit, -jnp.inf, masked)
        onehot = jnp.where(hit, 1.0, onehot)
        idx_out = jnp.where(lane == k, ei, idx_out)
        wgt_out = jnp.where(lane == k, wk, wgt_out)
        picks.append(hit)
    wsum = jnp.sum(wgt_out, axis=-1, keepdims=True)
    wgt_ref[...] = wgt_out / wsum * ROUTED_SCALE
    idx_ref[...] = idx_out

    r = lax.broadcasted_iota(I32, (tr, tr), 0)
    c = lax.broadcasted_iota(I32, (tr, tr), 1)
    before = jnp.where(r > c, 1.0, 0.0).astype(BF16)
    rank = carry_sc[...] + jnp.dot(before, onehot.astype(BF16), preferred_element_type=F32)
    rank_out = jnp.zeros((tr, LANE), F32)
    for k in range(TOP_K):
        rk = jnp.sum(jnp.where(picks[k], rank, 0.0), axis=-1, keepdims=True)
        rank_out = jnp.where(lane == k, rk, rank_out)
    rank_ref[...] = rank_out.astype(I32)
    carry_sc[...] += jnp.sum(onehot, axis=0, keepdims=True)
    cnt_ref[...] = carry_sc[...].astype(I32)


def router(x1b, w_router, router_bias, tr):
    rows, d = x1b.shape
    per_group = N_EXP // N_GROUP
    wr = jnp.pad(w_router, ((0, 0), (0, LANE - N_EXP))).astype(BF16)
    bias = jnp.pad(router_bias.astype(F32), (0, LANE - N_EXP)).reshape(1, LANE)
    e = np.arange(LANE)
    sel = np.stack([(e[:, None] == per_group * e[None, :] + r) & (e[None, :] < N_GROUP)
                    for r in range(per_group)]).astype(np.float32)
    expand = ((e[None, :] // per_group == e[:, None]) & (e[None, :] < N_EXP)).astype(np.float32)
    tile = pl.BlockSpec((tr, LANE), lambda i: (i, 0))
    return pl.pallas_call(
        _router_kernel,
        grid=(rows // tr,),
        in_specs=[pl.BlockSpec((tr, d), lambda i: (i, 0)),
                  pl.BlockSpec((d, LANE), lambda i: (0, 0)),
                  pl.BlockSpec((1, LANE), lambda i: (0, 0)),
                  pl.BlockSpec((per_group, LANE, LANE), lambda i: (0, 0, 0)),
                  pl.BlockSpec((LANE, LANE), lambda i: (0, 0))],
        out_specs=[tile, tile, tile, pl.BlockSpec((1, LANE), lambda i: (0, 0))],
        out_shape=[jax.ShapeDtypeStruct((rows, LANE), I32), jax.ShapeDtypeStruct((rows, LANE), F32),
                   jax.ShapeDtypeStruct((rows, LANE), I32), jax.ShapeDtypeStruct((1, LANE), I32)],
        scratch_shapes=[pltpu.VMEM((1, LANE), F32)],
        compiler_params=_cparams(("arbitrary",)),
        name="router",
    )(x1b, wr, bias, jnp.asarray(sel), jnp.asarray(expand))


GATHER_UNROLL = 8


def _gather_rows_kernel(idx_ref, nt_ref, src_ref, o_ref, buf, sem, *, ts):
    t = pl.program_id(0)
    base = t * ts

    @pl.when(t < nt_ref[0])
    def _():
        def issue(r, carry):
            pltpu.make_async_copy(src_ref.at[pl.ds(idx_ref[base + r], 1)], buf.at[pl.ds(r, 1)], sem).start()
            return carry

        lax.fori_loop(0, ts, issue, 0, unroll=GATHER_UNROLL)
        pltpu.make_async_copy(src_ref.at[pl.ds(0, ts)], buf, sem).wait()
        o_ref[...] = buf[...].astype(o_ref.dtype)

    @pl.when(t >= nt_ref[0])
    def _():
        o_ref[...] = jnp.zeros(o_ref.shape, o_ref.dtype)


def gather_rows(src, idx, n_tiles_used, ts, out_dtype):
    d = src.shape[1]
    n_out = idx.shape[0]
    grid_spec = pltpu.PrefetchScalarGridSpec(
        num_scalar_prefetch=2,
        grid=(n_out // ts,),
        in_specs=[pl.BlockSpec(memory_space=pl.ANY)],
        out_specs=pl.BlockSpec((ts, d), lambda t, idx, nt: (t, 0)),
        scratch_shapes=[pltpu.VMEM((ts, d), src.dtype), pltpu.SemaphoreType.DMA(())],
    )
    return pl.pallas_call(
        functools.partial(_gather_rows_kernel, ts=ts),
        grid_spec=grid_spec,
        out_shape=jax.ShapeDtypeStruct((n_out, d), out_dtype),
        compiler_params=_cparams(("arbitrary",)),
        name="gather_rows",
    )(idx, n_tiles_used, src)


def _expert_up_kernel(te_ref, nt_ref, x_ref, wa_ref, wb_ref, o_ref, wa_sc, wb_sc):
    t = pl.program_id(1)
    changed = jnp.logical_or(t == 0, te_ref[t] != te_ref[jnp.maximum(t - 1, 0)])

    @pl.when(jnp.logical_and(changed, t < nt_ref[0]))
    def _():
        wa_sc[...] = wa_ref[...].astype(BF16)
        wb_sc[...] = wb_ref[...].astype(BF16)

    @pl.when(t < nt_ref[0])
    def _():
        x = x_ref[...]
        a = jnp.dot(x, wa_sc[...], preferred_element_type=F32)
        b = jnp.dot(x, wb_sc[...], preferred_element_type=F32)
        o_ref[...] = (a * jax.nn.sigmoid(a) * b).astype(o_ref.dtype)

    @pl.when(t >= nt_ref[0])
    def _():
        o_ref[...] = jnp.zeros(o_ref.shape, o_ref.dtype)


def expert_up(xs, w_in, tile_expert, n_tiles_used, tm, fc):
    n_slots, d = xs.shape
    f = w_in.shape[2] // 2
    nf = f // fc

    def tmap(t, nt):
        return jnp.minimum(t, nt[0] - 1)

    grid_spec = pltpu.PrefetchScalarGridSpec(
        num_scalar_prefetch=2,
        grid=(nf, n_slots // tm),
        in_specs=[pl.BlockSpec((tm, d), lambda fi, t, te, nt: (tmap(t, nt), 0)),
                  pl.BlockSpec((None, d, fc), lambda fi, t, te, nt: (te[tmap(t, nt)], 0, fi)),
                  pl.BlockSpec((None, d, fc), lambda fi, t, te, nt: (te[tmap(t, nt)], 0, nf + fi))],
        out_specs=pl.BlockSpec((tm, fc), lambda fi, t, te, nt: (t, fi)),
        scratch_shapes=[pltpu.VMEM((d, fc), BF16), pltpu.VMEM((d, fc), BF16)],
    )
    return pl.pallas_call(
        _expert_up_kernel,
        grid_spec=grid_spec,
        out_shape=jax.ShapeDtypeStruct((n_slots, f), BF16),
        compiler_params=_cparams(("arbitrary", "arbitrary")),
        name="expert_up",
    )(tile_expert, n_tiles_used, xs, w_in, w_in)


def _expert_down_kernel(te_ref, nt_ref, h_ref, w_ref, o_ref, w_sc):
    t = pl.program_id(1)
    changed = jnp.logical_or(t == 0, te_ref[t] != te_ref[jnp.maximum(t - 1, 0)])

    @pl.when(jnp.logical_and(changed, t < nt_ref[0]))
    def _():
        w_sc[...] = w_ref[...].astype(BF16)

    @pl.when(t < nt_ref[0])
    def _():
        o_ref[...] = jnp.dot(h_ref[...], w_sc[...], preferred_element_type=F32)

    @pl.when(t >= nt_ref[0])
    def _():
        o_ref[...] = jnp.zeros(o_ref.shape, o_ref.dtype)


def expert_down(hs, w_out, tile_expert, n_tiles_used, tm, nc):
    n_slots, f = hs.shape
    d = w_out.shape[2]

    def tmap(t, nt):
        return jnp.minimum(t, nt[0] - 1)

    grid_spec = pltpu.PrefetchScalarGridSpec(
        num_scalar_prefetch=2,
        grid=(d // nc, n_slots // tm),
        in_specs=[pl.BlockSpec((tm, f), lambda j, t, te, nt: (tmap(t, nt), 0)),
                  pl.BlockSpec((None, f, nc), lambda j, t, te, nt: (te[tmap(t, nt)], 0, j))],
        out_specs=pl.BlockSpec((tm, nc), lambda j, t, te, nt: (t, j)),
        scratch_shapes=[pltpu.VMEM((f, nc), BF16)],
    )
    return pl.pallas_call(
        _expert_down_kernel,
        grid_spec=grid_spec,
        out_shape=jax.ShapeDtypeStruct((n_slots, d), F32),
        compiler_params=_cparams(("arbitrary", "arbitrary")),
        name="expert_down",
    )(tile_expert, n_tiles_used, hs, w_out)


def _swiglu_up_kernel(x_ref, wa_ref, wb_ref, o_ref):
    x = x_ref[...]
    a = jnp.dot(x, wa_ref[...].astype(BF16), preferred_element_type=F32)
    b = jnp.dot(x, wb_ref[...].astype(BF16), preferred_element_type=F32)
    o_ref[...] = (a * jax.nn.sigmoid(a) * b).astype(o_ref.dtype)


def swiglu_up(x, w_in, tm, fc):
    rows, d = x.shape
    f = w_in.shape[1] // 2
    nf = f // fc
    return pl.pallas_call(
        _swiglu_up_kernel,
        grid=(nf, rows // tm),
        in_specs=[pl.BlockSpec((tm, d), lambda fi, i: (i, 0)),
                  pl.BlockSpec((d, fc), lambda fi, i: (0, fi)),
                  pl.BlockSpec((d, fc), lambda fi, i: (0, nf + fi))],
        out_specs=pl.BlockSpec((tm, fc), lambda fi, i: (i, fi)),
        out_shape=jax.ShapeDtypeStruct((rows, f), BF16),
        compiler_params=_cparams(("parallel", "parallel")),
        name="swiglu_up",
    )(x, w_in, w_in)


def _combine_ln_kernel(slot_ref, y_ref, w_ref, sh_ref, x_ref, g_ref, b_ref, o_ref, buf, sem, *, tt):
    base = pl.program_id(0) * tt * TOP_K

    def issue(n, carry):
        for k in range(TOP_K):
            pltpu.make_async_copy(y_ref.at[pl.ds(slot_ref[base + n * TOP_K + k], 1)],
                                  buf.at[pl.ds(k * tt + n, 1)], sem).start()
        return carry

    lax.fori_loop(0, tt, issue, 0)
    pltpu.make_async_copy(y_ref.at[pl.ds(0, tt * TOP_K)], buf, sem).wait()
    ff = sh_ref[...]
    for k in range(TOP_K):
        ff = ff + w_ref[:, k:k + 1] * buf[k * tt:(k + 1) * tt, :]
    o_ref[...] = _layer_norm(DN_ALPHA * x_ref[...] + ff, g_ref[...], b_ref[...])


def combine_ln(y, slots, gates, shared, x1, ln_g, ln_b, tt):
    rows, d = x1.shape
    vec = pl.BlockSpec((1, d), lambda i, s: (0, 0))
    tile = pl.BlockSpec((tt, d), lambda i, s: (i, 0))
    grid_spec = pltpu.PrefetchScalarGridSpec(
        num_scalar_prefetch=1,
        grid=(rows // tt,),
        in_specs=[pl.BlockSpec(memory_space=pl.ANY), pl.BlockSpec((tt, LANE), lambda i, s: (i, 0)),
                  tile, tile, vec, vec],
        out_specs=tile,
        scratch_shapes=[pltpu.VMEM((tt * TOP_K, d), F32), pltpu.SemaphoreType.DMA(())],
    )
    return pl.pallas_call(
        functools.partial(_combine_ln_kernel, tt=tt),
        grid_spec=grid_spec,
        out_shape=jax.ShapeDtypeStruct((rows, d), F32),
        compiler_params=_cparams(("arbitrary",)),
        name="combine_ln",
    )(slots, y, gates, shared, x1, ln_g.reshape(1, d), ln_b.reshape(1, d))


def _largest_tile(n, candidates):
    for c in candidates:
        if n % c == 0:
            return c
    raise ValueError(f"no tile in {candidates} divides {n}")


def moe(x1, x1b, w_router, router_bias, w_exp_in, w_exp_out, w_sh_in, w_sh_out, ln_g, ln_b):
    rows, d = x1.shape
    f = w_exp_out.shape[1]
    tm = 256
    eidx_p, wgt_p, rank_p, cnt = router(x1b, w_router, router_bias, _largest_tile(rows, (256, 128, 64, 32, 16)))
    eidx, rank = eidx_p[:, :TOP_K], rank_p[:, :TOP_K]
    counts = cnt[0, :N_EXP]
    padded = (counts + tm - 1) // tm * tm
    ends = jnp.cumsum(padded)
    offs = ends - padded
    n_tiles = (rows * TOP_K + N_EXP * (tm - 1) + tm - 1) // tm
    n_slots = n_tiles * tm
    n_used = (ends[-1] // tm).astype(I32).reshape(1)
    tile_start = jnp.arange(n_tiles, dtype=I32) * tm
    tile_expert = jnp.minimum(jnp.sum(ends[None, :] <= tile_start[:, None], axis=1), N_EXP - 1).astype(I32)
    slot = (offs[eidx] + rank).astype(I32).reshape(-1)
    token = jnp.repeat(jnp.arange(rows, dtype=I32), TOP_K)
    token_of_slot = (jnp.arange(n_slots, dtype=I32) % rows).at[slot].set(token)

    xs = gather_rows(x1, token_of_slot, n_used, tm, BF16)
    hs = expert_up(xs, w_exp_in, tile_expert, n_used, tm, _largest_tile(f, (512, 256, 128)))
    ys = expert_down(hs, w_exp_out, tile_expert, n_used, tm, _largest_tile(d, (2048, 1024, 512, 256)))
    t_sh = _largest_tile(rows, (512, 256, 128, 64, 32, 16))
    h_sh = swiglu_up(x1b, w_sh_in, t_sh, _largest_tile(w_sh_in.shape[1] // 2, (512, 256, 128)))
    y_sh = matmul(h_sh, w_sh_out, t_sh, _largest_tile(d, (512, 256)))
    return combine_ln(ys, slot, wgt_p, y_sh, x1, ln_g, ln_b, _largest_tile(rows, (64, 32, 16, 8)))


def kernel(x_prompt, mem_prompt, x_sample, cache_k, cache_v, cache_idx_k, state_hgrn, cache_mem_k, cache_mem_v,
           page_table, w_in, w_idx_qb, idx_ln_g, idx_ln_b, hgrn_lb_logits, hgrn_norm_g, w_mem_kv, w_proj_a,
           w_proj_b, w_proj_m, w_o, ln1_g, ln1_b, w_router, router_bias, w_exp_in, w_exp_out, w_sh_in,
           w_sh_out, ln2_g, ln2_b):
    assert x_prompt.shape[0] == 1 and w_in.shape[0] == DEPTH
    _, L, D = x_prompt.shape
    B, T, _ = x_sample.shape
    RS = B * T
    R = L + RS
    page = cache_k.shape[2]
    n_pages = page_table.shape[1]
    past = n_pages * page
    n_mem = mem_prompt.shape[1]

    x = jnp.concatenate([x_prompt[0], x_sample.reshape(RS, D)], axis=0)
    n_a = C_WIDX + H_IDX
    w_pad = jnp.concatenate([w_in[0][:, :n_a].astype(BF16), jnp.zeros((D, LANE - H_IDX), BF16),
                             w_in[0][:, n_a:].astype(BF16)], axis=1)
    t_row = _largest_tile(R, (512, 256, 128, 64, 32, 16))
    z = matmul(x.astype(BF16), w_pad, t_row, _largest_tile(w_pad.shape[1], (640, 512, 256, 128)))
    pos = jnp.concatenate([jnp.arange(L, dtype=I32), past + jnp.arange(RS, dtype=I32) % T])
    tables = _rope_tables(pos)
    t_prep = _largest_tile(R, (256, 128))
    q_rot, k_rot, k_rot_b, v, v_t, k_idx, k_idx_b = prep_qkv(z, tables, idx_ln_g[0], idx_ln_b[0], t_prep)
    q_idx = qidx_proj(z, w_idx_qb[0].astype(BF16), tables, t_prep)
    w_idx = z[:, C_WIDX:C_WIDX + H_IDX]

    kc_p = _largest_tile(L, (1024, 512, 256, 128))
    scores_p = idx_scores_prompt(q_idx, (w_idx * IDX_W_SCALE).T, k_idx_b, L, _largest_tile(L, (256, 128)), kc_p, LANE)
    bias_p = select_bias(scores_p, jnp.arange(L, dtype=I32), min(TOPK_MAX, L // 4), LANE, min(kc_p, 512))
    o_a_p = attn_prompt(q_rot, k_rot_b, v_t, bias_p, L, _largest_tile(L, (256, 128)),
                        _largest_tile(L, (512, 256, 128)))

    q_s = q_idx[:, L:, :].reshape(H_IDX, B, T, D_IDX).transpose(1, 2, 0, 3).reshape(B, T * H_IDX, D_IDX)
    pad_new = lambda a: jnp.zeros((B, page, a.shape[-1]), BF16).at[:, :T].set(
        a.astype(BF16).reshape(B, T, a.shape[-1]))
    pad_q = lambda a, nh, dh: jnp.zeros((B, nh, Q_ROWS, dh), BF16).at[:, :, :T].set(
        a.astype(BF16).reshape(B, T, nh, dh).transpose(0, 2, 1, 3))
    scores_s = idx_scores_sample(q_s, w_idx[L:].reshape(B, T * H_IDX, 1), cache_idx_k[0], pad_new(k_idx_b[L:]),
                                 page_table)
    width_s = past + page
    lim_s = past + jnp.arange(RS, dtype=I32) % T
    bias_s = select_bias(scores_s.reshape(RS, width_s).T, lim_s, min(TOPK_MAX, (past + T) // 4),
                         _largest_tile(RS, (128, 64, 32, 16)), LANE)
    bias_s = jnp.zeros((B, Q_ROWS, width_s), F32).at[:, :T].set(bias_s.T.reshape(B, T, width_s))
    pages = lambda c: c[0].reshape(c.shape[1], page * H_A, DH_A)
    o_a_s = attn_sample(pad_q(q_rot[L:], H_A, DH_A), pages(cache_k), pages(cache_v), pad_new(k_rot_b[L:]),
                        pad_new(v[L:]), bias_s[:, :, :past], bias_s[:, :, past:], page_table, T)

    lb = jnp.cumsum(jax.nn.softmax(hgrn_lb_logits.astype(F32), axis=0), axis=0)[0]
    o_b_p, s_p = hgrn_prompt(z, L, lb, hgrn_norm_g[0], _largest_tile(L, (512, 256, 128)), LANE, 32)
    o_b_s, s_s = hgrn_sample(z[L:, C_FB:C_QM], state_hgrn[0], lb, hgrn_norm_g[0], T, SUBLANE // T)

    mkv = matmul(mem_prompt[0].astype(BF16), w_mem_kv[0].astype(BF16),
                 _largest_tile(n_mem, (256, 128, 64, 32, 16)), _largest_tile(2 * W_M, (512,)))
    mk, mv = mkv[:, :W_M], mkv[:, W_M:]
    o_m_p = mem_attn_prompt(z, L, mk.astype(BF16), mv.astype(BF16), _largest_tile(L, (512, 256, 128)))
    qm_bd = jnp.einsum("btgd,hg->bhtgd", z[L:, C_QM:C_QM + W_M].astype(BF16).reshape(B, T, H_M, DH_M),
                       jnp.eye(H_M, dtype=BF16)).reshape(B, H_M * T, W_M)
    o_m_s = mem_attn_sample(qm_bd, cache_mem_k[0].reshape(B, n_mem, W_M), cache_mem_v[0].reshape(B, n_mem, W_M), T)

    o_a = jnp.concatenate([o_a_p, o_a_s.reshape(RS, W_A).astype(BF16)], axis=0)
    o_b = jnp.concatenate([o_b_p, o_b_s.astype(BF16)], axis=0)
    o_m = jnp.concatenate([o_m_p, o_m_s.reshape(RS, W_M).astype(BF16)], axis=0)
    h = gated_proj(o_a, o_b, o_m, w_proj_a[0].astype(BF16), w_proj_b[0].astype(BF16), w_proj_m[0].astype(BF16),
                   z, D, t_row, 256)
    x1, x1b = out_proj_ln(h, w_o[0].astype(BF16), x, ln1_g[0], ln1_b[0], t_prep, _largest_tile(D, (512, 256)))
    y = moe(x1, x1b, w_router[0], router_bias[0], w_exp_in[0], w_exp_out[0], w_sh_in[0], w_sh_out[0],
            ln2_g[0], ln2_b[0])

    return (y[:L].reshape(1, L, D), y[L:].reshape(B, T, D),
            k_rot[:L].reshape(1, 1, L, H_A, DH_A), v[:L].reshape(1, 1, L, H_A, DH_A),
            k_idx[:L].reshape(1, 1, L, D_IDX), s_p.reshape(1, 1, H_B, DK_B, DV_B),
            mk.reshape(1, 1, n_mem, H_M, DH_M), mv.reshape(1, 1, n_mem, H_M, DH_M),
            k_rot[L:].reshape(1, B, T, H_A, DH_A), v[L:].reshape(1, B, T, H_A, DH_A),
            k_idx[L:].reshape(1, B, T, D_IDX), s_s.reshape(1, B, H_B, DK_B, DV_B))
```

```python
import functools

import jax
import jax.numpy as jnp
import numpy as np
from jax import lax
from jax.experimental import pallas as pl
from jax.experimental.pallas import tpu as pltpu

F32 = jnp.float32
BF16 = jnp.bfloat16
I32 = jnp.int32

H_A = 16
DH_A = 128
W_A = H_A * DH_A
H_IDX = 32
D_IDX = 128
R_IDX = 512
IDX_ROPE = 64
IDX_W_SCALE = (H_IDX ** -0.5) * (D_IDX ** -0.5)
TOPK_MAX = 256
H_B = 8
DK_B = 128
DV_B = 128
W_B = H_B * DV_B
H_M = 4
DH_M = 256
W_M = H_M * DH_M
N_EXP = 64
TOP_K = 8
N_GROUP = 8
TOPK_GROUP = 4
ROUTED_SCALE = 2.5
ROPE_THETA = 10000.0
LN_EPS = 1e-5
N_BRANCH = 3
DEPTH = 1
DN_ALPHA = (2.0 * DEPTH) ** 0.25

LANE = 128
SUBLANE = 8
Q_ROWS = 16
NEG = -1e30
VMEM_LIMIT = 56 * 1024 * 1024

C_Q = 0
C_K = W_A
C_V = 2 * W_A
C_CIDX = 3 * W_A
C_KIDX = C_CIDX + R_IDX
C_WIDX = C_KIDX + D_IDX
C_FB = C_WIDX + LANE
C_IB = C_FB + H_B * DK_B
C_QB = C_IB + W_B
C_GB = C_QB + H_B * DK_B
C_QM = C_GB + W_B
C_GATE = C_QM + W_M

NT = (((1,), (1,)), ((), ()))


def _cparams(sem, vmem=VMEM_LIMIT):
    return pltpu.CompilerParams(dimension_semantics=sem, vmem_limit_bytes=vmem)


def _mm_kernel(x_ref, w_ref, o_ref):
    o_ref[...] = jnp.dot(x_ref[...], w_ref[...].astype(BF16),
                         preferred_element_type=F32).astype(o_ref.dtype)


def matmul(x, w, tm, tn, out_dtype=F32):
    m, k = x.shape
    n = w.shape[1]
    return pl.pallas_call(
        _mm_kernel,
        grid=(m // tm, n // tn),
        in_specs=[pl.BlockSpec((tm, k), lambda i, j: (i, 0)),
                  pl.BlockSpec((k, tn), lambda i, j: (0, j))],
        out_specs=pl.BlockSpec((tm, tn), lambda i, j: (i, j)),
        out_shape=jax.ShapeDtypeStruct((m, n), out_dtype),
        compiler_params=_cparams(("parallel", "parallel")),
        name="matmul",
    )(x, w)


def _rope_tables(pos):
    posf = pos.astype(F32)[:, None]
    half = DH_A // 2
    inv = ROPE_THETA ** (-jnp.arange(half, dtype=F32) / half)
    ang = posf * inv[None, :]
    c, s = jnp.cos(ang), jnp.sin(ang)
    cos_a = jnp.concatenate([c, c], axis=-1)
    sin_a = jnp.concatenate([-s, s], axis=-1)
    half_i = IDX_ROPE // 2
    inv_i = ROPE_THETA ** (-jnp.arange(half_i, dtype=F32) / half_i)
    ang_i = posf * inv_i[None, :]
    ci, si = jnp.cos(ang_i), jnp.sin(ang_i)
    one = jnp.ones((pos.shape[0], D_IDX - IDX_ROPE), F32)
    zero_h = jnp.zeros_like(si)
    zero_r = jnp.zeros_like(one)
    cos_i = jnp.concatenate([ci, ci, one], axis=-1)
    sin_dn = jnp.concatenate([zero_h, si, zero_r], axis=-1)
    sin_up = jnp.concatenate([-si, zero_h, zero_r], axis=-1)
    return cos_a, sin_a, cos_i, sin_dn, sin_up


def _rope_full(x, cos, sin):
    return x * cos + pltpu.roll(x, DH_A // 2, 1) * sin


def _rope_idx(x, cos, sin_dn, sin_up):
    h = IDX_ROPE // 2
    return x * cos + pltpu.roll(x, h, 1) * sin_dn + pltpu.roll(x, D_IDX - h, 1) * sin_up


def _prep_kernel(q_ref, k_ref, v_ref, ki_ref, cos_ref, sin_ref, cosi_ref, sdn_ref, sup_ref, g_ref, b_ref,
                 qo_ref, ko_ref, kb_ref, vo_ref, vt_ref, kio_ref, kib_ref):
    cos = cos_ref[...]
    sin = sin_ref[...]
    for h in range(H_A):
        sl = slice(h * DH_A, (h + 1) * DH_A)
        qo_ref[:, sl] = _rope_full(q_ref[:, sl], cos, sin).astype(qo_ref.dtype)
        kr = _rope_full(k_ref[:, sl], cos, sin)
        ko_ref[:, sl] = kr
        kb_ref[:, sl] = kr.astype(kb_ref.dtype)
        v = v_ref[:, sl]
        vo_ref[:, sl] = v
        vt_ref[sl, :] = v.T.astype(vt_ref.dtype)
    x = ki_ref[...]
    mu = jnp.mean(x, axis=-1, keepdims=True)
    var = jnp.mean(jnp.square(x - mu), axis=-1, keepdims=True)
    xn = (x - mu) * lax.rsqrt(var + LN_EPS) * g_ref[...] + b_ref[...]
    ki = _rope_idx(xn, cosi_ref[...], sdn_ref[...], sup_ref[...])
    kio_ref[...] = ki
    kib_ref[...] = ki.astype(kib_ref.dtype)


def prep_qkv(z, tables, ln_g, ln_b, tr):
    rows = z.shape[0]
    cos_a, sin_a, cos_i, sin_dn, sin_up = tables
    row_blk = lambda w, c: pl.BlockSpec((tr, w), lambda i: (i, c))
    tab = pl.BlockSpec((tr, LANE), lambda i: (i, 0))
    vec = pl.BlockSpec((1, LANE), lambda i: (0, 0))
    wide = pl.BlockSpec((tr, W_A), lambda i: (i, 0))
    narrow = pl.BlockSpec((tr, D_IDX), lambda i: (i, 0))
    return pl.pallas_call(
        _prep_kernel,
        grid=(rows // tr,),
        in_specs=[row_blk(W_A, C_Q // W_A), row_blk(W_A, C_K // W_A), row_blk(W_A, C_V // W_A),
                  row_blk(D_IDX, C_KIDX // D_IDX), tab, tab, tab, tab, tab, vec, vec],
        out_specs=[wide, wide, wide, wide, pl.BlockSpec((W_A, tr), lambda i: (0, i)), narrow, narrow],
        out_shape=[jax.ShapeDtypeStruct((rows, W_A), BF16),
                   jax.ShapeDtypeStruct((rows, W_A), F32),
                   jax.ShapeDtypeStruct((rows, W_A), BF16),
                   jax.ShapeDtypeStruct((rows, W_A), F32),
                   jax.ShapeDtypeStruct((W_A, rows), BF16),
                   jax.ShapeDtypeStruct((rows, D_IDX), F32),
                   jax.ShapeDtypeStruct((rows, D_IDX), BF16)],
        compiler_params=_cparams(("parallel",)),
        name="prep_qkv",
    )(z, z, z, z, cos_a, sin_a, cos_i, sin_dn, sin_up, ln_g.reshape(1, D_IDX), ln_b.reshape(1, D_IDX))


def _qidx_kernel(c_ref, w_ref, cosi_ref, sdn_ref, sup_ref, o_ref):
    acc = jnp.dot(c_ref[...].astype(BF16), w_ref[...], preferred_element_type=F32)
    cos, sdn, sup = cosi_ref[...], sdn_ref[...], sup_ref[...]
    for h in range(H_IDX):
        o_ref[h] = _rope_idx(acc[:, h * D_IDX:(h + 1) * D_IDX], cos, sdn, sup).astype(o_ref.dtype)


def qidx_proj(z, w_qb, tables, tr):
    rows = z.shape[0]
    _, _, cos_i, sin_dn, sin_up = tables
    tab = pl.BlockSpec((tr, LANE), lambda i: (i, 0))
    return pl.pallas_call(
        _qidx_kernel,
        grid=(rows // tr,),
        in_specs=[pl.BlockSpec((tr, R_IDX), lambda i: (i, C_CIDX // R_IDX)),
                  pl.BlockSpec((R_IDX, H_IDX * D_IDX), lambda i: (0, 0)),
                  tab, tab, tab],
        out_specs=pl.BlockSpec((H_IDX, tr, D_IDX), lambda i: (0, i, 0)),
        out_shape=jax.ShapeDtypeStruct((H_IDX, rows, D_IDX), BF16),
        compiler_params=_cparams(("parallel",)),
        name="qidx_proj",
    )(z, w_qb, cos_i, sin_dn, sin_up)


def _idx_score_prompt_kernel(q_ref, w_ref, k_ref, o_ref, *, tq, kc, ks):
    i, j = pl.program_id(0), pl.program_id(1)

    @pl.when(j * kc <= i * tq + tq - 1)
    def _():
        for u in range(kc // ks):
            k = k_ref[u * ks:(u + 1) * ks, :]
            acc = jnp.zeros((ks, tq), F32)
            for h in range(H_IDX):
                s = lax.dot_general(k, q_ref[h], NT, preferred_element_type=F32)
                acc = acc + jnp.maximum(s, 0.0) * w_ref[h:h + 1, :]
            o_ref[u * ks:(u + 1) * ks, :] = acc

    @pl.when(j * kc > i * tq + tq - 1)
    def _():
        o_ref[...] = jnp.zeros(o_ref.shape, F32)


def idx_scores_prompt(q_idx, w_t, k_idx_b, L, tq, kc, ks):
    def kmap(i, j):
        return (jnp.minimum(j, (i * tq + tq - 1) // kc), 0)

    return pl.pallas_call(
        functools.partial(_idx_score_prompt_kernel, tq=tq, kc=kc, ks=ks),
        grid=(L // tq, L // kc),
        in_specs=[pl.BlockSpec((H_IDX, tq, D_IDX), lambda i, j: (0, i, 0)),
                  pl.BlockSpec((H_IDX, tq), lambda i, j: (0, i)),
                  pl.BlockSpec((kc, D_IDX), kmap)],
        out_specs=pl.BlockSpec((kc, tq), lambda i, j: (j, i)),
        out_shape=jax.ShapeDtypeStruct((L, L), F32),
        compiler_params=_cparams(("parallel", "arbitrary")),
        name="idx_scores_prompt",
    )(q_idx, w_t, k_idx_b)


def _idx_score_sample_kernel(pt_ref, q_ref, w_ref, *refs, n_pages, n_new):
    page_refs = refs[:n_pages]
    knew_ref = refs[n_pages]
    o_ref = refs[n_pages + 1]
    q = q_ref[...]
    w = w_ref[...] * IDX_W_SCALE
    page = page_refs[0].shape[0]

    def score(k):
        s = lax.dot_general(q, k, NT, preferred_element_type=F32)
        s = jnp.maximum(s, 0.0) * w
        return jnp.sum(s.reshape(n_new, H_IDX, k.shape[0]), axis=1)

    for p in range(n_pages):
        o_ref[:, p * page:(p + 1) * page] = score(page_refs[p][...].astype(BF16))
    o_ref[:, n_pages * page:] = score(knew_ref[...])


def idx_scores_sample(q_s, w_s, cache_idx, knew_idx, page_table):
    nb, rows, _ = q_s.shape
    n_new = rows // H_IDX
    n_pages = page_table.shape[1]
    page = cache_idx.shape[1]
    page_specs = [pl.BlockSpec((None, page, D_IDX), functools.partial(lambda b, pt, p: (pt[b, p], 0, 0), p=p))
                  for p in range(n_pages)]
    grid_spec = pltpu.PrefetchScalarGridSpec(
        num_scalar_prefetch=1,
        grid=(nb,),
        in_specs=[pl.BlockSpec((None, rows, D_IDX), lambda b, pt: (b, 0, 0)),
                  pl.BlockSpec((None, rows, 1), lambda b, pt: (b, 0, 0))] + page_specs +
                 [pl.BlockSpec((None, page, D_IDX), lambda b, pt: (b, 0, 0))],
        out_specs=pl.BlockSpec((None, n_new, (n_pages + 1) * page), lambda b, pt: (b, 0, 0)),
    )
    return pl.pallas_call(
        functools.partial(_idx_score_sample_kernel, n_pages=n_pages, n_new=n_new),
        grid_spec=grid_spec,
        out_shape=jax.ShapeDtypeStruct((nb, n_new, (n_pages + 1) * page), F32),
        compiler_params=_cparams(("arbitrary",)),
        name="idx_scores_sample",
    )(page_table, q_s, w_s, *([cache_idx] * n_pages), knew_idx)


def _key_to_float(key):
    bits = key ^ (jnp.right_shift(key, 31) & jnp.int32(0x7FFFFFFF))
    return lax.bitcast_convert_type(bits, F32)


def _select_kernel(nch_ref, lim_ref, s_ref, o_ref, m_ref, *, n_sel, kc):
    width, tq = s_ref.shape
    n_chunks = nch_ref[pl.program_id(0)]
    lim = lim_ref[...]
    kq = jnp.minimum(lim + 1, n_sel).astype(F32)

    def mask_chunk(c, carry):
        rows = pl.ds(pl.multiple_of(c * kc, kc), kc)
        row = c * kc + lax.broadcasted_iota(I32, (kc, tq), 0)
        m_ref[rows, :] = jnp.where(row <= lim, s_ref[rows, :], -jnp.inf)
        return carry

    lax.fori_loop(0, n_chunks, mask_chunk, 0)

    acc_rows = min(kc, 8 * SUBLANE)

    def count_ge(thr):
        def body(c, acc):
            blk = m_ref[pl.ds(pl.multiple_of(c * kc, kc), kc), :]
            hit = jnp.where(blk >= thr, 1.0, 0.0)
            return acc + jnp.sum(hit.reshape(kc // acc_rows, acc_rows, tq), axis=0)

        acc = lax.fori_loop(0, n_chunks, body, jnp.zeros((acc_rows, tq), F32))
        return jnp.sum(acc, axis=0, keepdims=True)

    def bit_step(it, key):
        cand = key + jnp.left_shift(jnp.int32(1), 31 - it)
        ok = count_ge(_key_to_float(cand)) >= kq
        return jnp.where(ok, cand, key)

    key = lax.fori_loop(0, 32, bit_step, jnp.full((1, tq), jnp.iinfo(jnp.int32).min, I32))
    thr = _key_to_float(key)

    def out_chunk(c, carry):
        rows = pl.ds(pl.multiple_of(c * kc, kc), kc)
        o_ref[rows, :] = jnp.where(m_ref[rows, :] >= thr, 0.0, NEG)
        return carry

    lax.fori_loop(0, n_chunks, out_chunk, 0)

    def fill_chunk(c, carry):
        o_ref[pl.ds(pl.multiple_of(c * kc, kc), kc), :] = jnp.full((kc, tq), NEG, F32)
        return carry

    lax.fori_loop(n_chunks, width // kc, fill_chunk, 0)


def select_bias(scores_t, lim, n_sel, tq, kc):
    width, n_q = scores_t.shape
    n_tiles = n_q // tq
    nch = (jnp.max(lim.reshape(n_tiles, tq), axis=1) + kc) // kc
    grid_spec = pltpu.PrefetchScalarGridSpec(
        num_scalar_prefetch=1,
        grid=(n_tiles,),
        in_specs=[pl.BlockSpec((1, tq), lambda i, n: (0, i)),
                  pl.BlockSpec((width, tq), lambda i, n: (0, i))],
        out_specs=pl.BlockSpec((width, tq), lambda i, n: (0, i)),
        scratch_shapes=[pltpu.VMEM((width, tq), F32)],
    )
    return pl.pallas_call(
        functools.partial(_select_kernel, n_sel=n_sel, kc=kc),
        grid_spec=grid_spec,
        out_shape=jax.ShapeDtypeStruct((width, n_q), F32),
        compiler_params=_cparams(("arbitrary",)),
        name="select_bias",
    )(nch.astype(I32), lim.reshape(1, n_q).astype(I32), scores_t)


def _attn_prompt_kernel(q_ref, k_ref, vt_ref, b_ref, o_ref, m_sc, l_sc, acc_sc, *, tq, tk):
    i, j = pl.program_id(0), pl.program_id(1)
    last = (i * tq + tq - 1) // tk
    scale = DH_A ** -0.5

    @pl.when(j == 0)
    def _():
        m_sc[...] = jnp.full(m_sc.shape, NEG, F32)
        l_sc[...] = jnp.zeros(l_sc.shape, F32)
        acc_sc[...] = jnp.zeros(acc_sc.shape, F32)

    @pl.when(j <= last)
    def _():
        for h in range(H_A):
            sl = slice(h * DH_A, (h + 1) * DH_A)
            q_h = q_ref[:, sl]
            m, l, acc = m_sc[h], l_sc[h], acc_sc[h]
            for u in range(tk // LANE):
                keys = slice(u * LANE, (u + 1) * LANE)
                s = lax.dot_general(k_ref[keys, sl], q_h, NT, preferred_element_type=F32) * scale + b_ref[keys, :]
                m_new = jnp.maximum(m, jnp.max(s, axis=0, keepdims=True))
                alpha = jnp.exp(m - m_new)
                p = jnp.exp(s - m_new)
                l = alpha * l + jnp.sum(p, axis=0, keepdims=True)
                acc = alpha * acc + jnp.dot(vt_ref[sl, keys], p.astype(BF16), preferred_element_type=F32)
                m = m_new
            m_sc[h], l_sc[h], acc_sc[h] = m, l, acc

    @pl.when(j == last)
    def _():
        for h in range(H_A):
            o_ref[:, h * DH_A:(h + 1) * DH_A] = (acc_sc[h] / l_sc[h]).T.astype(o_ref.dtype)


def attn_prompt(q, k, v_t, bias_t, L, tq, tk):
    def jmap(i, j):
        return jnp.minimum(j, (i * tq + tq - 1) // tk)

    return pl.pallas_call(
        functools.partial(_attn_prompt_kernel, tq=tq, tk=tk),
        grid=(L // tq, L // tk),
        in_specs=[pl.BlockSpec((tq, W_A), lambda i, j: (i, 0)),
                  pl.BlockSpec((tk, W_A), lambda i, j: (jmap(i, j), 0)),
                  pl.BlockSpec((W_A, tk), lambda i, j: (0, jmap(i, j))),
                  pl.BlockSpec((tk, tq), lambda i, j: (jmap(i, j), i))],
        out_specs=pl.BlockSpec((tq, W_A), lambda i, j: (i, 0)),
        out_shape=jax.ShapeDtypeStruct((L, W_A), BF16),
        scratch_shapes=[pltpu.VMEM((H_A, 1, tq), F32), pltpu.VMEM((H_A, 1, tq), F32),
                        pltpu.VMEM((H_A, DH_A, tq), F32)],
        compiler_params=_cparams(("parallel", "arbitrary")),
        name="attn_prompt",
    )(q, k, v_t, bias_t)


def _attn_sample_kernel(pt_ref, q_ref, *refs, n_steps, group, n_new):
    kc_refs, vc_refs = refs[:group], refs[group:2 * group]
    kn_ref, vn_ref, b_ref, bn_ref, o_ref, m_sc, l_sc, acc_sc = refs[2 * group:]
    p = pl.program_id(1)
    scale = DH_A ** -0.5

    @pl.when(p == 0)
    def _():
        m_sc[...] = jnp.full(m_sc.shape, NEG, F32)
        l_sc[...] = jnp.zeros(l_sc.shape, F32)
        acc_sc[...] = jnp.zeros(acc_sc.shape, F32)

    def update(keys, values, bias):
        s = jnp.stack([lax.dot_general(q_ref[h], keys(h), NT, preferred_element_type=F32) for h in range(H_A)])
        s = s * scale + bias[None]
        m_prev = m_sc[...]
        m_new = jnp.maximum(m_prev, jnp.max(s, axis=-1, keepdims=True))
        alpha = jnp.exp(m_prev - m_new)
        pr = jnp.exp(s - m_new)
        l_sc[...] = alpha * l_sc[...] + jnp.sum(pr, axis=-1, keepdims=True)
        pb = pr.astype(BF16)
        pv = jnp.stack([jnp.dot(pb[h], values(h), preferred_element_type=F32) for h in range(H_A)])
        acc_sc[...] = alpha * acc_sc[...] + pv
        m_sc[...] = m_new

    @pl.when(p < n_steps)
    def _():
        page = kc_refs[0].shape[0] // H_A
        for g in range(group):
            update(lambda h, g=g: kc_refs[g][pl.ds(h, page, stride=H_A), :].astype(BF16),
                   lambda h, g=g: vc_refs[g][pl.ds(h, page, stride=H_A), :].astype(BF16),
                   b_ref[:, g * page:(g + 1) * page])

    @pl.when(p == n_steps)
    def _():
        update(lambda h: kn_ref[:, h * DH_A:(h + 1) * DH_A], lambda h: vn_ref[:, h * DH_A:(h + 1) * DH_A],
               bn_ref[...])
        out = acc_sc[...] / l_sc[...]
        for h in range(H_A):
            o_ref[:, h * DH_A:(h + 1) * DH_A] = out[h, :n_new]


def attn_sample(q_h, cache_k, cache_v, knew, vnew, bias_past, bias_new, page_table, n_new):
    nb = q_h.shape[0]
    n_pages = page_table.shape[1]
    page = cache_k.shape[1] // H_A
    group = _largest_tile(n_pages, (4, 2, 1))
    n_steps = n_pages // group

    def cmap(g):
        return lambda b, p, pt: (pt[b, jnp.minimum(p, n_steps - 1) * group + g], 0, 0)

    new_spec = pl.BlockSpec((None, page, W_A), lambda b, p, pt: (b, 0, 0))
    cache_specs = [pl.BlockSpec((None, page * H_A, DH_A), cmap(g)) for g in range(group)]
    grid_spec = pltpu.PrefetchScalarGridSpec(
        num_scalar_prefetch=1,
        grid=(nb, n_steps + 1),
        in_specs=[pl.BlockSpec((None, H_A, Q_ROWS, DH_A), lambda b, p, pt: (b, 0, 0, 0))] +
                 cache_specs + cache_specs + [new_spec, new_spec,
                  pl.BlockSpec((None, Q_ROWS, group * page), lambda b, p, pt: (b, 0, jnp.minimum(p, n_steps - 1))),
                  pl.BlockSpec((None, Q_ROWS, page), lambda b, p, pt: (b, 0, 0))],
        out_specs=pl.BlockSpec((None, n_new, W_A), lambda b, p, pt: (b, 0, 0)),
        scratch_shapes=[pltpu.VMEM((H_A, Q_ROWS, 1), F32), pltpu.VMEM((H_A, Q_ROWS, 1), F32),
                        pltpu.VMEM((H_A, Q_ROWS, DH_A), F32)],
    )
    return pl.pallas_call(
        functools.partial(_attn_sample_kernel, n_steps=n_steps, group=group, n_new=n_new),
        grid_spec=grid_spec,
        out_shape=jax.ShapeDtypeStruct((nb, n_new, W_A), F32),
        compiler_params=_cparams(("parallel", "arbitrary")),
        name="attn_sample",
    )(page_table, q_h, *([cache_k] * group), *([cache_v] * group), knew, vnew, bias_past, bias_new)


def _hgrn_chunk(fr, ic, qc, lb, st, sub, row_valid=None):
    C = fr.shape[0]
    lf = jnp.log(lb + (1.0 - lb) * jax.nn.sigmoid(fr))
    kk = (1.0 - lb) * jax.nn.sigmoid(-fr)
    if row_valid is not None:
        lf = jnp.where(row_valid, lf, 0.0)
        kk = jnp.where(row_valid, kk, 0.0)
    r = lax.broadcasted_iota(I32, (C, C), 0)
    c = lax.broadcasted_iota(I32, (C, C), 1)
    tri = jnp.where(r >= c, 1.0, 0.0).astype(F32)
    G = jnp.dot(tri, lf, preferred_element_type=F32, precision=lax.Precision.HIGHEST)
    o_inter = lax.dot_general((qc * jnp.exp(G)).astype(BF16), st.astype(BF16), NT, preferred_element_type=F32)
    ic_b = ic.astype(BF16)
    t3 = lax.broadcasted_iota(I32, (sub, sub, DK_B), 0)
    s3 = lax.broadcasted_iota(I32, (sub, sub, DK_B), 1)
    pieces = []
    for a in range(C // sub):
        r0 = a * sub
        Gs, ks, qs, is_ = G[r0:r0 + sub], kk[r0:r0 + sub], qc[r0:r0 + sub], ic[r0:r0 + sub]
        dec = jnp.exp(jnp.where(t3 >= s3, Gs[:, None, :] - Gs[None, :, :], -jnp.inf))
        a3 = jnp.sum(dec * ks[None, :, :] * qs[:, None, :], axis=-1, keepdims=True)
        o_a = jnp.sum(a3 * is_[None, :, :], axis=1)
        if a > 0:
            gb = G[r0 - 1:r0]
            qa = (qs * jnp.exp(Gs - gb)).astype(BF16)
            kb = (kk[:r0] * jnp.exp(gb - G[:r0])).astype(BF16)
            A = lax.dot_general(qa, kb, NT, preferred_element_type=F32)
            o_a = o_a + jnp.dot(A.astype(BF16), ic_b[:r0], preferred_element_type=F32)
        pieces.append(o_a)
    o = o_inter + (pieces[0] if len(pieces) == 1 else jnp.concatenate(pieces, axis=0))
    g_last = G[C - 1:C]
    kd = (kk * jnp.exp(g_last - G)).astype(BF16)
    st_new = st * jnp.exp(g_last) + lax.dot_general(ic_b, kd, (((0,), (0,)), ((), ())),
                                                    preferred_element_type=F32)
    return o, st_new


def _hgrn_finish(o, gate, norm_g):
    o = o * lax.rsqrt(jnp.mean(jnp.square(o), axis=-1, keepdims=True) + LN_EPS)
    return o * norm_g * (gate * jax.nn.sigmoid(gate))


def _hgrn_prompt_kernel(f_ref, i_ref, q_ref, g_ref, lb_ref, ng_ref, o_ref, s_ref, st_sc, *, chunk, sub):
    j = pl.program_id(1)

    @pl.when(j == 0)
    def _():
        st_sc[...] = jnp.zeros(st_sc.shape, F32)

    lb = lb_ref[...]
    ng = ng_ref[...]

    def body(c, carry):
        rows = pl.ds(pl.multiple_of(c * chunk, chunk), chunk)
        o, st_new = _hgrn_chunk(f_ref[rows, :], i_ref[rows, :], q_ref[rows, :], lb, st_sc[...], sub)
        st_sc[...] = st_new
        o_ref[rows, :] = _hgrn_finish(o, g_ref[rows, :], ng).astype(o_ref.dtype)
        return carry

    lax.fori_loop(0, f_ref.shape[0] // chunk, body, 0)

    @pl.when(j == pl.num_programs(1) - 1)
    def _():
        s_ref[...] = st_sc[...].T


def hgrn_prompt(z, n_rows, lb, norm_g, tb, chunk, sub):
    blk = lambda c0: pl.BlockSpec((tb, DK_B), lambda h, j, c0=c0: (j, c0 // DK_B + h))
    vec = pl.BlockSpec((1, DK_B), lambda h, j: (0, h))
    return pl.pallas_call(
        functools.partial(_hgrn_prompt_kernel, chunk=chunk, sub=sub),
        grid=(H_B, n_rows // tb),
        in_specs=[blk(C_FB), blk(C_IB), blk(C_QB), blk(C_GB), vec,
                  pl.BlockSpec((1, DV_B), lambda h, j: (0, 0))],
        out_specs=[pl.BlockSpec((tb, DV_B), lambda h, j: (j, h)),
                   pl.BlockSpec((None, DK_B, DV_B), lambda h, j: (h, 0, 0))],
        out_shape=[jax.ShapeDtypeStruct((n_rows, W_B), BF16),
                   jax.ShapeDtypeStruct((H_B, DK_B, DV_B), F32)],
        scratch_shapes=[pltpu.VMEM((DV_B, DK_B), F32)],
        compiler_params=_cparams(("parallel", "arbitrary")),
        name="hgrn_prompt",
    )(z, z, z, z, lb.reshape(1, H_B * DK_B), norm_g.reshape(1, DV_B))


def _hgrn_sample_kernel(x_ref, s_ref, lb_ref, ng_ref, o_ref, so_ref, *, n_new, seqs):
    rows = seqs * n_new
    rid = lax.broadcasted_iota(I32, (rows, DK_B), 0)
    ng = ng_ref[...]
    for h in range(H_B):
        col = lambda g: slice(g * W_B + h * DK_B, g * W_B + (h + 1) * DK_B)
        fr, ic, qc, gate = x_ref[:, col(0)], x_ref[:, col(1)], x_ref[:, col(2)], x_ref[:, col(3)]
        lb = lb_ref[:, h * DK_B:(h + 1) * DK_B]
        o = jnp.zeros((rows, DV_B), F32)
        for b in range(seqs):
            valid = (rid >= b * n_new) & (rid < (b + 1) * n_new)
            ob, st_new = _hgrn_chunk(fr, ic, qc, lb, s_ref[b, h].T, rows, row_valid=valid)
            so_ref[b, h] = st_new.T
            o = jnp.where(valid, ob, o)
        o_ref[:, h * DV_B:(h + 1) * DV_B] = _hgrn_finish(o, gate, ng).astype(o_ref.dtype)


def hgrn_sample(xs, state, lb, norm_g, n_new, seqs):
    nb = state.shape[0]
    rows = seqs * n_new
    return pl.pallas_call(
        functools.partial(_hgrn_sample_kernel, n_new=n_new, seqs=seqs),
        grid=(nb // seqs,),
        in_specs=[pl.BlockSpec((rows, 4 * W_B), lambda i: (i, 0)),
                  pl.BlockSpec((seqs, H_B, DK_B, DV_B), lambda i: (i, 0, 0, 0)),
                  pl.BlockSpec((1, H_B * DK_B), lambda i: (0, 0)),
                  pl.BlockSpec((1, DV_B), lambda i: (0, 0))],
        out_specs=[pl.BlockSpec((rows, W_B), lambda i: (i, 0)),
                   pl.BlockSpec((seqs, H_B, DK_B, DV_B), lambda i: (i, 0, 0, 0))],
        out_shape=[jax.ShapeDtypeStruct((nb * n_new, W_B), F32),
                   jax.ShapeDtypeStruct(state.shape, F32)],
        compiler_params=_cparams(("parallel",)),
        name="hgrn_sample",
    )(xs, state, lb.reshape(1, H_B * DK_B), norm_g.reshape(1, DV_B))


def _softmax_rows(s):
    m = jnp.max(s, axis=-1, keepdims=True)
    p = jnp.exp(s - m)
    return p / jnp.sum(p, axis=-1, keepdims=True)


def _mem_attn_prompt_kernel(*refs):
    q_refs, mk_ref, mv_ref, o_ref = refs[:H_M], refs[H_M], refs[H_M + 1], refs[H_M + 2]
    for h in range(H_M):
        sl = slice(h * DH_M, (h + 1) * DH_M)
        s = lax.dot_general(q_refs[h][...].astype(BF16), mk_ref[:, sl], NT,
                            preferred_element_type=F32) * (DH_M ** -0.5)
        p = _softmax_rows(s).astype(BF16)
        o_ref[:, sl] = jnp.dot(p, mv_ref[:, sl], preferred_element_type=F32).astype(o_ref.dtype)


def mem_attn_prompt(z, n_rows, mk, mv, tr):
    n_mem = mk.shape[0]
    q_specs = [pl.BlockSpec((tr, DH_M), functools.partial(lambda i, h: (i, C_QM // DH_M + h), h=h))
               for h in range(H_M)]
    kv = pl.BlockSpec((n_mem, W_M), lambda i: (0, 0))
    return pl.pallas_call(
        _mem_attn_prompt_kernel,
        grid=(n_rows // tr,),
        in_specs=q_specs + [kv, kv],
        out_specs=pl.BlockSpec((tr, W_M), lambda i: (i, 0)),
        out_shape=jax.ShapeDtypeStruct((n_rows, W_M), BF16),
        compiler_params=_cparams(("parallel",)),
        name="mem_attn_prompt",
    )(*([z] * H_M), mk, mv)


def _mem_attn_sample_kernel(q_ref, k_ref, v_ref, o_ref, *, n_new):
    s = lax.dot_general(q_ref[...], k_ref[...].astype(BF16), NT,
                        preferred_element_type=F32) * (DH_M ** -0.5)
    p = _softmax_rows(s).astype(BF16)
    full = jnp.dot(p, v_ref[...].astype(BF16), preferred_element_type=F32)
    for h in range(H_M):
        o_ref[:, h * DH_M:(h + 1) * DH_M] = full[h * n_new:(h + 1) * n_new, h * DH_M:(h + 1) * DH_M]


def mem_attn_sample(q_bd, cache_mk, cache_mv, n_new):
    nb, rows, _ = q_bd.shape
    n_mem = cache_mk.shape[1]
    return pl.pallas_call(
        functools.partial(_mem_attn_sample_kernel, n_new=n_new),
        grid=(nb,),
        in_specs=[pl.BlockSpec((None, rows, W_M), lambda b: (b, 0, 0)),
                  pl.BlockSpec((None, n_mem, W_M), lambda b: (b, 0, 0)),
                  pl.BlockSpec((None, n_mem, W_M), lambda b: (b, 0, 0))],
        out_specs=pl.BlockSpec((None, n_new, W_M), lambda b: (b, 0, 0)),
        out_shape=jax.ShapeDtypeStruct((nb, n_new, W_M), F32),
        compiler_params=_cparams(("parallel",)),
        name="mem_attn_sample",
    )(q_bd, cache_mk, cache_mv)


def _gated_proj_kernel(oa_ref, ob_ref, om_ref, wa_ref, wb_ref, wm_ref, g0_ref, g1_ref, g2_ref, o_ref):
    def branch(x_ref, w_ref, g_ref):
        return jax.nn.sigmoid(g_ref[...]) * jnp.dot(x_ref[...], w_ref[...], preferred_element_type=F32)

    h = branch(oa_ref, wa_ref, g0_ref) + branch(ob_ref, wb_ref, g1_ref) + branch(om_ref, wm_ref, g2_ref)
    o_ref[...] = h.astype(o_ref.dtype)


def gated_proj(o_a, o_b, o_m, w_a, w_b, w_m, z, d_model, tm, tn):
    rows = o_a.shape[0]
    x_spec = lambda w: pl.BlockSpec((tm, w), lambda i, j: (i, 0))
    w_spec = lambda w: pl.BlockSpec((w, tn), lambda i, j: (0, j))
    g_spec = lambda g: pl.BlockSpec((tm, tn), lambda i, j, g=g: (i, (C_GATE + g * d_model) // tn + j))
    return pl.pallas_call(
        _gated_proj_kernel,
        grid=(rows // tm, d_model // tn),
        in_specs=[x_spec(W_A), x_spec(W_B), x_spec(W_M), w_spec(W_A), w_spec(W_B), w_spec(W_M),
                  g_spec(0), g_spec(1), g_spec(2)],
        out_specs=pl.BlockSpec((tm, tn), lambda i, j: (i, j)),
        out_shape=jax.ShapeDtypeStruct((rows, d_model), BF16),
        compiler_params=_cparams(("parallel", "parallel")),
        name="gated_proj",
    )(o_a, o_b, o_m, w_a, w_b, w_m, z, z, z)


def _layer_norm(x, g, b):
    mu = jnp.mean(x, axis=-1, keepdims=True)
    var = jnp.mean(jnp.square(x - mu), axis=-1, keepdims=True)
    return (x - mu) * lax.rsqrt(var + LN_EPS) * g + b


def _out_proj_ln_kernel(h_ref, w_ref, x_ref, g_ref, b_ref, o_ref, ob_ref, acc_sc):
    k = pl.program_id(1)

    @pl.when(k == 0)
    def _():
        acc_sc[...] = jnp.zeros(acc_sc.shape, F32)

    acc_sc[...] += jnp.dot(h_ref[...], w_ref[...], preferred_element_type=F32)

    @pl.when(k == pl.num_programs(1) - 1)
    def _():
        y = _layer_norm(DN_ALPHA * x_ref[...] + acc_sc[...], g_ref[...], b_ref[...])
        o_ref[...] = y
        ob_ref[...] = y.astype(ob_ref.dtype)


def out_proj_ln(h, w_o, x, ln_g, ln_b, tm, tk):
    rows, d = x.shape
    vec = pl.BlockSpec((1, d), lambda i, k: (0, 0))
    return pl.pallas_call(
        _out_proj_ln_kernel,
        grid=(rows // tm, d // tk),
        in_specs=[pl.BlockSpec((tm, tk), lambda i, k: (i, k)),
                  pl.BlockSpec((tk, d), lambda i, k: (k, 0)),
                  pl.BlockSpec((tm, d), lambda i, k: (i, 0)), vec, vec],
        out_specs=[pl.BlockSpec((tm, d), lambda i, k: (i, 0)),
                   pl.BlockSpec((tm, d), lambda i, k: (i, 0))],
        out_shape=[jax.ShapeDtypeStruct((rows, d), F32), jax.ShapeDtypeStruct((rows, d), BF16)],
        scratch_shapes=[pltpu.VMEM((tm, d), F32)],
        compiler_params=_cparams(("parallel", "arbitrary")),
        name="out_proj_ln",
    )(h, w_o, x, ln_g.reshape(1, d), ln_b.reshape(1, d))


def _router_kernel(x_ref, w_ref, b_ref, sel_ref, exp_ref, idx_ref, wgt_ref, rank_ref, cnt_ref, carry_sc):
    i = pl.program_id(0)

    @pl.when(i == 0)
    def _():
        carry_sc[...] = jnp.zeros(carry_sc.shape, F32)

    tr = x_ref.shape[0]
    per_group = N_EXP // N_GROUP
    lane = lax.broadcasted_iota(I32, (tr, LANE), 1)
    s = jax.nn.sigmoid(jnp.dot(x_ref[...], w_ref[...], preferred_element_type=F32))
    real = lane < N_EXP
    choice = jnp.where(real, s + b_ref[...], 0.0)
    hi = lax.Precision.HIGHEST

    def first_argmax(v):
        mx = jnp.max(v, axis=-1, keepdims=True)
        return mx, jnp.min(jnp.where(v == mx, lane, LANE), axis=-1, keepdims=True)

    members = [jnp.dot(choice, sel_ref[r], preferred_element_type=F32, precision=hi) for r in range(per_group)]
    m1 = functools.reduce(jnp.maximum, members)
    first = functools.reduce(jnp.minimum, [jnp.where(members[r] == m1, r, per_group) for r in range(per_group)])
    m2 = functools.reduce(jnp.maximum, [jnp.where(first == r, -jnp.inf, members[r]) for r in range(per_group)])
    cur = jnp.where(lane < N_GROUP, m1 + m2, -jnp.inf)
    gsel = jnp.zeros((tr, LANE), F32)
    for _ in range(TOPK_GROUP):
        _, gi = first_argmax(cur)
        hit = lane == gi
        gsel = jnp.where(hit, 1.0, gsel)
        cur = jnp.where(hit, -jnp.inf, cur)
    emask = jnp.dot(gsel, exp_ref[...], preferred_element_type=F32, precision=hi)
    masked = jnp.where((emask > 0.5) & real, choice, -jnp.inf)

    idx_out = jnp.zeros((tr, LANE), I32)
    wgt_out = jnp.zeros((tr, LANE), F32)
    onehot = jnp.zeros((tr, LANE), F32)
    picks = []
    for k in range(TOP_K):
        _, ei = first_argmax(masked)
        hit = lane == ei
        wk = jnp.sum(jnp.where(hit, s, 0.0), axis=-1, keepdims=True)
        masked = jnp.where(hit, -jnp.inf, masked)
        onehot = jnp.where(hit, 1.0, onehot)
        idx_out = jnp.where(lane == k, ei, idx_out)
        wgt_out = jnp.where(lane == k, wk, wgt_out)
        picks.append(hit)
    wsum = jnp.sum(wgt_out, axis=-1, keepdims=True)
    wgt_ref[...] = wgt_out / wsum * ROUTED_SCALE
    idx_ref[...] = idx_out

    r = lax.broadcasted_iota(I32, (tr, tr), 0)
    c = lax.broadcasted_iota(I32, (tr, tr), 1)
    before = jnp.where(r > c, 1.0, 0.0).astype(BF16)
    rank = carry_sc[...] + jnp.dot(before, onehot.astype(BF16), preferred_element_type=F32)
    rank_out = jnp.zeros((tr, LANE), F32)
    for k in range(TOP_K):
        rk = jnp.sum(jnp.where(picks[k], rank, 0.0), axis=-1, keepdims=True)
        rank_out = jnp.where(lane == k, rk, rank_out)
    rank_ref[...] = rank_out.astype(I32)
    carry_sc[...] += jnp.sum(onehot, axis=0, keepdims=True)
    cnt_ref[...] = carry_sc[...].astype(I32)


def router(x1b, w_router, router_bias, tr):
    rows, d = x1b.shape
    per_group = N_EXP // N_GROUP
    wr = jnp.pad(w_router, ((0, 0), (0, LANE - N_EXP))).astype(BF16)
    bias = jnp.pad(router_bias.astype(F32), (0, LANE - N_EXP)).reshape(1, LANE)
    e = np.arange(LANE)
    sel = np.stack([(e[:, None] == per_group * e[None, :] + r) & (e[None, :] < N_GROUP)
                    for r in range(per_group)]).astype(np.float32)
    expand = ((e[None, :] // per_group == e[:, None]) & (e[None, :] < N_EXP)).astype(np.float32)
    tile = pl.BlockSpec((tr, LANE), lambda i: (i, 0))
    return pl.pallas_call(
        _router_kernel,
        grid=(rows // tr,),
        in_specs=[pl.BlockSpec((tr, d), lambda i: (i, 0)),
                  pl.BlockSpec((d, LANE), lambda i: (0, 0)),
                  pl.BlockSpec((1, LANE), lambda i: (0, 0)),
                  pl.BlockSpec((per_group, LANE, LANE), lambda i: (0, 0, 0)),
                  pl.BlockSpec((LANE, LANE), lambda i: (0, 0))],
        out_specs=[tile, tile, tile, pl.BlockSpec((1, LANE), lambda i: (0, 0))],
        out_shape=[jax.ShapeDtypeStruct((rows, LANE), I32), jax.ShapeDtypeStruct((rows, LANE), F32),
                   jax.ShapeDtypeStruct((rows, LANE), I32), jax.ShapeDtypeStruct((1, LANE), I32)],
        scratch_shapes=[pltpu.VMEM((1, LANE), F32)],
        compiler_params=_cparams(("arbitrary",)),
        name="router",
    )(x1b, wr, bias, jnp.asarray(sel), jnp.asarray(expand))


GATHER_UNROLL = 8


def _gather_rows_kernel(idx_ref, nt_ref, src_ref, o_ref, buf, sem, *, ts):
    t = pl.program_id(0)
    base = t * ts

    @pl.when(t < nt_ref[0])
    def _():
        def issue(r, carry):
            pltpu.make_async_copy(src_ref.at[pl.ds(idx_ref[base + r], 1)], buf.at[pl.ds(r, 1)], sem).start()
            return carry

        lax.fori_loop(0, ts, issue, 0, unroll=GATHER_UNROLL)
        pltpu.make_async_copy(src_ref.at[pl.ds(0, ts)], buf, sem).wait()
        o_ref[...] = buf[...].astype(o_ref.dtype)

    @pl.when(t >= nt_ref[0])
    def _():
        o_ref[...] = jnp.zeros(o_ref.shape, o_ref.dtype)


def gather_rows(src, idx, n_tiles_used, ts, out_dtype):
    d = src.shape[1]
    n_out = idx.shape[0]
    grid_spec = pltpu.PrefetchScalarGridSpec(
        num_scalar_prefetch=2,
        grid=(n_out // ts,),
        in_specs=[pl.BlockSpec(memory_space=pl.ANY)],
        out_specs=pl.BlockSpec((ts, d), lambda t, idx, nt: (t, 0)),
        scratch_shapes=[pltpu.VMEM((ts, d), src.dtype), pltpu.SemaphoreType.DMA(())],
    )
    return pl.pallas_call(
        functools.partial(_gather_rows_kernel, ts=ts),
        grid_spec=grid_spec,
        out_shape=jax.ShapeDtypeStruct((n_out, d), out_dtype),
        compiler_params=_cparams(("arbitrary",)),
        name="gather_rows",
    )(idx, n_tiles_used, src)


def _expert_up_kernel(te_ref, nt_ref, x_ref, wa_ref, wb_ref, o_ref, wa_sc, wb_sc):
    t = pl.program_id(1)
    changed = jnp.logical_or(t == 0, te_ref[t] != te_ref[jnp.maximum(t - 1, 0)])

    @pl.when(jnp.logical_and(changed, t < nt_ref[0]))
    def _():
        wa_sc[...] = wa_ref[...].astype(BF16)
        wb_sc[...] = wb_ref[...].astype(BF16)

    @pl.when(t < nt_ref[0])
    def _():
        x = x_ref[...]
        a = jnp.dot(x, wa_sc[...], preferred_element_type=F32)
        b = jnp.dot(x, wb_sc[...], preferred_element_type=F32)
        o_ref[...] = (a * jax.nn.sigmoid(a) * b).astype(o_ref.dtype)

    @pl.when(t >= nt_ref[0])
    def _():
        o_ref[...] = jnp.zeros(o_ref.shape, o_ref.dtype)


def expert_up(xs, w_in, tile_expert, n_tiles_used, tm, fc):
    n_slots, d = xs.shape
    f = w_in.shape[2] // 2
    nf = f // fc

    def tmap(t, nt):
        return jnp.minimum(t, nt[0] - 1)

    grid_spec = pltpu.PrefetchScalarGridSpec(
        num_scalar_prefetch=2,
        grid=(nf, n_slots // tm),
        in_specs=[pl.BlockSpec((tm, d), lambda fi, t, te, nt: (tmap(t, nt), 0)),
                  pl.BlockSpec((None, d, fc), lambda fi, t, te, nt: (te[tmap(t, nt)], 0, fi)),
                  pl.BlockSpec((None, d, fc), lambda fi, t, te, nt: (te[tmap(t, nt)], 0, nf + fi))],
        out_specs=pl.BlockSpec((tm, fc), lambda fi, t, te, nt: (t, fi)),
        scratch_shapes=[pltpu.VMEM((d, fc), BF16), pltpu.VMEM((d, fc), BF16)],
    )
    return pl.pallas_call(
        _expert_up_kernel,
        grid_spec=grid_spec,
        out_shape=jax.ShapeDtypeStruct((n_slots, f), BF16),
        compiler_params=_cparams(("arbitrary", "arbitrary")),
        name="expert_up",
    )(tile_expert, n_tiles_used, xs, w_in, w_in)


def _expert_down_kernel(te_ref, nt_ref, h_ref, w_ref, o_ref, w_sc):
    t = pl.program_id(1)
    changed = jnp.logical_or(t == 0, te_ref[t] != te_ref[jnp.maximum(t - 1, 0)])

    @pl.when(jnp.logical_and(changed, t < nt_ref[0]))
    def _():
        w_sc[...] = w_ref[...].astype(BF16)

    @pl.when(t < nt_ref[0])
    def _():
        o_ref[...] = jnp.dot(h_ref[...], w_sc[...], preferred_element_type=F32)

    @pl.when(t >= nt_ref[0])
    def _():
        o_ref[...] = jnp.zeros(o_ref.shape, o_ref.dtype)


def expert_down(hs, w_out, tile_expert, n_tiles_used, tm, nc):
    n_slots, f = hs.shape
    d = w_out.shape[2]

    def tmap(t, nt):
        return jnp.minimum(t, nt[0] - 1)

    grid_spec = pltpu.PrefetchScalarGridSpec(
        num_scalar_prefetch=2,
        grid=(d // nc, n_slots // tm),
        in_specs=[pl.BlockSpec((tm, f), lambda j, t, te, nt: (tmap(t, nt), 0)),
                  pl.BlockSpec((None, f, nc), lambda j, t, te, nt: (te[tmap(t, nt)], 0, j))],
        out_specs=pl.BlockSpec((tm, nc), lambda j, t, te, nt: (t, j)),
        scratch_shapes=[pltpu.VMEM((f, nc), BF16)],
    )
    return pl.pallas_call(
        _expert_down_kernel,
        grid_spec=grid_spec,
        out_shape=jax.ShapeDtypeStruct((n_slots, d), F32),
        compiler_params=_cparams(("arbitrary", "arbitrary")),
        name="expert_down",
    )(tile_expert, n_tiles_used, hs, w_out)


def _swiglu_up_kernel(x_ref, wa_ref, wb_ref, o_ref):
    x = x_ref[...]
    a = jnp.dot(x, wa_ref[...].astype(BF16), preferred_element_type=F32)
    b = jnp.dot(x, wb_ref[...].astype(BF16), preferred_element_type=F32)
    o_ref[...] = (a * jax.nn.sigmoid(a) * b).astype(o_ref.dtype)


def swiglu_up(x, w_in, tm, fc):
    rows, d = x.shape
    f = w_in.shape[1] // 2
    nf = f // fc
    return pl.pallas_call(
        _swiglu_up_kernel,
        grid=(nf, rows // tm),
        in_specs=[pl.BlockSpec((tm, d), lambda fi, i: (i, 0)),
                  pl.BlockSpec((d, fc), lambda fi, i: (0, fi)),
                  pl.BlockSpec((d, fc), lambda fi, i: (0, nf + fi))],
        out_specs=pl.BlockSpec((tm, fc), lambda fi, i: (i, fi)),
        out_shape=jax.ShapeDtypeStruct((rows, f), BF16),
        compiler_params=_cparams(("parallel", "parallel")),
        name="swiglu_up",
    )(x, w_in, w_in)


def _combine_ln_kernel(slot_ref, y_ref, w_ref, sh_ref, x_ref, g_ref, b_ref, o_ref, buf, sem, *, tt):
    base = pl.program_id(0) * tt * TOP_K

    def issue(n, carry):
        for k in range(TOP_K):
            pltpu.make_async_copy(y_ref.at[pl.ds(slot_ref[base + n * TOP_K + k], 1)],
                                  buf.at[pl.ds(k * tt + n, 1)], sem).start()
        return carry

    lax.fori_loop(0, tt, issue, 0)
    pltpu.make_async_copy(y_ref.at[pl.ds(0, tt * TOP_K)], buf, sem).wait()
    ff = sh_ref[...]
    for k in range(TOP_K):
        ff = ff + w_ref[:, k:k + 1] * buf[k * tt:(k + 1) * tt, :]
    o_ref[...] = _layer_norm(DN_ALPHA * x_ref[...] + ff, g_ref[...], b_ref[...])


def combine_ln(y, slots, gates, shared, x1, ln_g, ln_b, tt):
    rows, d = x1.shape
    vec = pl.BlockSpec((1, d), lambda i, s: (0, 0))
    tile = pl.BlockSpec((tt, d), lambda i, s: (i, 0))
    grid_spec = pltpu.PrefetchScalarGridSpec(
        num_scalar_prefetch=1,
        grid=(rows // tt,),
        in_specs=[pl.BlockSpec(memory_space=pl.ANY), pl.BlockSpec((tt, LANE), lambda i, s: (i, 0)),
                  tile, tile, vec, vec],
        out_specs=tile,
        scratch_shapes=[pltpu.VMEM((tt * TOP_K, d), F32), pltpu.SemaphoreType.DMA(())],
    )
    return pl.pallas_call(
        functools.partial(_combine_ln_kernel, tt=tt),
        grid_spec=grid_spec,
        out_shape=jax.ShapeDtypeStruct((rows, d), F32),
        compiler_params=_cparams(("arbitrary",)),
        name="combine_ln",
    )(slots, y, gates, shared, x1, ln_g.reshape(1, d), ln_b.reshape(1, d))


def _largest_tile(n, candidates):
    for c in candidates:
        if n % c == 0:
            return c
    raise ValueError(f"no tile in {candidates} divides {n}")


def moe(x1, x1b, w_router, router_bias, w_exp_in, w_exp_out, w_sh_in, w_sh_out, ln_g, ln_b):
    rows, d = x1.shape
    f = w_exp_out.shape[1]
    tm = 256
    eidx_p, wgt_p, rank_p, cnt = router(x1b, w_router, router_bias, _largest_tile(rows, (256, 128, 64, 32, 16)))
    eidx, rank = eidx_p[:, :TOP_K], rank_p[:, :TOP_K]
    counts = cnt[0, :N_EXP]
    padded = (counts + tm - 1) // tm * tm
    ends = jnp.cumsum(padded)
    offs = ends - padded
    n_tiles = (rows * TOP_K + N_EXP * (tm - 1) + tm - 1) // tm
    n_slots = n_tiles * tm
    n_used = (ends[-1] // tm).astype(I32).reshape(1)
    tile_start = jnp.arange(n_tiles, dtype=I32) * tm
    tile_expert = jnp.minimum(jnp.sum(ends[None, :] <= tile_start[:, None], axis=1), N_EXP - 1).astype(I32)
    slot = (offs[eidx] + rank).astype(I32).reshape(-1)
    token = jnp.repeat(jnp.arange(rows, dtype=I32), TOP_K)
    token_of_slot = (jnp.arange(n_slots, dtype=I32) % rows).at[slot].set(token)

    xs = gather_rows(x1, token_of_slot, n_used, tm, BF16)
    hs = expert_up(xs, w_exp_in, tile_expert, n_used, tm, _largest_tile(f, (512, 256, 128)))
    ys = expert_down(hs, w_exp_out, tile_expert, n_used, tm, _largest_tile(d, (2048, 1024, 512, 256)))
    t_sh = _largest_tile(rows, (512, 256, 128, 64, 32, 16))
    h_sh = swiglu_up(x1b, w_sh_in, t_sh, _largest_tile(w_sh_in.shape[1] // 2, (512, 256, 128)))
    y_sh = matmul(h_sh, w_sh_out, t_sh, _largest_tile(d, (512, 256)))
    return combine_ln(ys, slot, wgt_p, y_sh, x1, ln_g, ln_b, _largest_tile(rows, (64, 32, 16, 8)))


def kernel(x_prompt, mem_prompt, x_sample, cache_k, cache_v, cache_idx_k, state_hgrn, cache_mem_k, cache_mem_v,
           page_table, w_in, w_idx_qb, idx_ln_g, idx_ln_b, hgrn_lb_logits, hgrn_norm_g, w_mem_kv, w_proj_a,
           w_proj_b, w_proj_m, w_o, ln1_g, ln1_b, w_router, router_bias, w_exp_in, w_exp_out, w_sh_in,
           w_sh_out, ln2_g, ln2_b):
    assert x_prompt.shape[0] == 1 and w_in.shape[0] == DEPTH
    _, L, D = x_prompt.shape
    B, T, _ = x_sample.shape
    RS = B * T
    R = L + RS
    page = cache_k.shape[2]
    n_pages = page_table.shape[1]
    past = n_pages * page
    n_mem = mem_prompt.shape[1]

    x = jnp.concatenate([x_prompt[0], x_sample.reshape(RS, D)], axis=0)
    n_a = C_WIDX + H_IDX
    w_pad = jnp.concatenate([w_in[0][:, :n_a].astype(BF16), jnp.zeros((D, LANE - H_IDX), BF16),
                             w_in[0][:, n_a:].astype(BF16)], axis=1)
    t_row = _largest_tile(R, (1088, 512, 256, 128))
    z = matmul(x.astype(BF16), w_pad, t_row, _largest_tile(w_pad.shape[1], (640, 512, 256, 128)))
    pos = jnp.concatenate([jnp.arange(L, dtype=I32), past + jnp.arange(RS, dtype=I32) % T])
    tables = _rope_tables(pos)
    t_prep = _largest_tile(R, (256, 128))
    q_rot, k_rot, k_rot_b, v, v_t, k_idx, k_idx_b = prep_qkv(z, tables, idx_ln_g[0], idx_ln_b[0], t_prep)
    q_idx = qidx_proj(z, w_idx_qb[0].astype(BF16), tables, t_prep)
    w_idx = z[:, C_WIDX:C_WIDX + H_IDX]

    kc_p = _largest_tile(L, (1024, 512, 256, 128))
    scores_p = idx_scores_prompt(q_idx, (w_idx * IDX_W_SCALE).T, k_idx_b, L, _largest_tile(L, (256, 128)), kc_p, LANE)
    bias_p = select_bias(scores_p, jnp.arange(L, dtype=I32), min(TOPK_MAX, L // 4), LANE, min(kc_p, 512))
    o_a_p = attn_prompt(q_rot, k_rot_b, v_t, bias_p, L, _largest_tile(L, (256, 128)),
                        _largest_tile(L, (512, 256, 128)))

    q_s = q_idx[:, L:, :].reshape(H_IDX, B, T, D_IDX).transpose(1, 2, 0, 3).reshape(B, T * H_IDX, D_IDX)
    pad_new = lambda a: jnp.zeros((B, page, a.shape[-1]), BF16).at[:, :T].set(
        a.astype(BF16).reshape(B, T, a.shape[-1]))
    pad_q = lambda a, nh, dh: jnp.zeros((B, nh, Q_ROWS, dh), BF16).at[:, :, :T].set(
        a.astype(BF16).reshape(B, T, nh, dh).transpose(0, 2, 1, 3))
    scores_s = idx_scores_sample(q_s, w_idx[L:].reshape(B, T * H_IDX, 1), cache_idx_k[0], pad_new(k_idx_b[L:]),
                                 page_table)
    width_s = past + page
    lim_s = past + jnp.arange(RS, dtype=I32) % T
    bias_s = select_bias(scores_s.reshape(RS, width_s).T, lim_s, min(TOPK_MAX, (past + T) // 4),
                         _largest_tile(RS, (128, 64, 32, 16)), LANE)
    bias_s = jnp.zeros((B, Q_ROWS, width_s), F32).at[:, :T].set(bias_s.T.reshape(B, T, width_s))
    pages = lambda c: c[0].reshape(c.shape[1], page * H_A, DH_A)
    o_a_s = attn_sample(pad_q(q_rot[L:], H_A, DH_A), pages(cache_k), pages(cache_v), pad_new(k_rot_b[L:]),
                        pad_new(v[L:]), bias_s[:, :, :past], bias_s[:, :, past:], page_table, T)

    lb = jnp.cumsum(jax.nn.softmax(hgrn_lb_logits.astype(F32), axis=0), axis=0)[0]
    o_b_p, s_p = hgrn_prompt(z, L, lb, hgrn_norm_g[0], _largest_tile(L, (512, 256, 128)), LANE, 32)
    o_b_s, s_s = hgrn_sample(z[L:, C_FB:C_QM], state_hgrn[0], lb, hgrn_norm_g[0], T, SUBLANE // T)

    mkv = matmul(mem_prompt[0].astype(BF16), w_mem_kv[0].astype(BF16),
                 _largest_tile(n_mem, (256, 128, 64, 32, 16)), _largest_tile(2 * W_M, (512,)))
    mk, mv = mkv[:, :W_M], mkv[:, W_M:]
    o_m_p = mem_attn_prompt(z, L, mk.astype(BF16), mv.astype(BF16), _largest_tile(L, (512, 256, 128)))
    qm_bd = jnp.einsum("btgd,hg->bhtgd", z[L:, C_QM:C_QM + W_M].astype(BF16).reshape(B, T, H_M, DH_M),
                       jnp.eye(H_M, dtype=BF16)).reshape(B, H_M * T, W_M)
    o_m_s = mem_attn_sample(qm_bd, cache_mem_k[0].reshape(B, n_mem, W_M), cache_mem_v[0].reshape(B, n_mem, W_M), T)

    o_a = jnp.concatenate([o_a_p, o_a_s.reshape(RS, W_A).astype(BF16)], axis=0)
    o_b = jnp.concatenate([o_b_p, o_b_s.astype(BF16)], axis=0)
    o_m = jnp.concatenate([o_m_p, o_m_s.reshape(RS, W_M).astype(BF16)], axis=0)
    h = gated_proj(o_a, o_b, o_m, w_proj_a[0].astype(BF16), w_proj_b[0].astype(BF16), w_proj_m[0].astype(BF16),
                   z, D, t_row, 256)
    x1, x1b = out_proj_ln(h, w_o[0].astype(BF16), x, ln1_g[0], ln1_b[0], t_prep, _largest_tile(D, (1024, 512, 256)))
    y = moe(x1, x1b, w_router[0], router_bias[0], w_exp_in[0], w_exp_out[0], w_sh_in[0], w_sh_out[0],
            ln2_g[0], ln2_b[0])

    return (y[:L].reshape(1, L, D), y[L:].reshape(B, T, D),
            k_rot[:L].reshape(1, 1, L, H_A, DH_A), v[:L].reshape(1, 1, L, H_A, DH_A),
            k_idx[:L].reshape(1, 1, L, D_IDX), s_p.reshape(1, 1, H_B, DK_B, DV_B),
            mk.reshape(1, 1, n_mem, H_M, DH_M), mv.reshape(1, 1, n_mem, H_M, DH_M),
            k_rot[L:].reshape(1, B, T, H_A, DH_A), v[L:].reshape(1, B, T, H_A, DH_A),
            k_idx[L:].reshape(1, B, T, D_IDX), s_s.reshape(1, B, H_B, DK_B, DV_B))
```
